```python
import math
import jax
import jax.numpy as jnp
from jax import lax
import numpy as np

D_MODEL = 2048
BATCH = 2
SEQ = 4096
DEPTH = 4

N_MIXERS = 2
GRID_W = 64
N_META = 16
CHUNK = 128
N_PAD = CHUNK - N_META
EPS = 1e-6
D_FF = 3 * D_MODEL
SSD_EXPAND = 2
D_INNER = SSD_EXPAND * D_MODEL
SSD_HEAD_DIM = 64
SSD_HEADS = D_INNER // SSD_HEAD_DIM
SSD_GROUPS = 8
SSD_HPG = SSD_HEADS // SSD_GROUPS
SSD_STATE = 128
SSD_CONV = 7
SSD_CONV_CH = D_INNER + 2 * SSD_GROUPS * SSD_STATE
SSD_PROJ = D_INNER + SSD_CONV_CH + 2 * SSD_HEADS
HEAD_DIM = 128
N_HEADS = D_MODEL // HEAD_DIM
N_KV_HEADS = N_HEADS // 2
GQ = N_HEADS // N_KV_HEADS
ROPE_AXIS_DIM = HEAD_DIM // 2
ROPE_HALF = ROPE_AXIS_DIM // 2
ROPE_THETA = 10000.0
Q_BLOCK = 128
QKV_W = (N_HEADS + 2 * N_KV_HEADS) * HEAD_DIM

kernel_name = "hybrid_ssd_gqa_macaron_encoder"


def rms_norm(x, w):
    xf = x.astype(jnp.float32)
    y = xf * lax.rsqrt(jnp.mean(xf * xf, axis=-1, keepdims=True) + EPS)
    return (y * w.astype(jnp.float32)).astype(x.dtype)


def swiglu_ffn(x, norm_w, w_in, w_out):
    h = rms_norm(x, norm_w)
    gate, up = jnp.split(h @ w_in, 2, axis=-1)
    return (jax.nn.silu(gate) * up) @ w_out


def depthwise_conv_centred(u, w, b):
    k = w.shape[0]
    out = lax.conv_general_dilated(
        u, w[:, None, :].astype(u.dtype), window_strides=(1,),
        padding=[((k - 1) // 2, (k - 1) // 2)],
        dimension_numbers=("NWC", "WIO", "NWC"),
        feature_group_count=u.shape[-1])
    return out + b


def ssd_chunked(xdt, a, bm, cm):
    b, L, G, HG, P = xdt.shape
    nc = L // CHUNK
    xdt = xdt.reshape(b, nc, CHUNK, G, HG, P)
    bm = bm.reshape(b, nc, CHUNK, G, SSD_STATE)
    cm = cm.reshape(b, nc, CHUNK, G, SSD_STATE)
    a = a.astype(jnp.float32).reshape(b, nc, CHUNK, G, HG).transpose(0, 3, 4, 1, 2)
    a_cs = jnp.cumsum(a, axis=-1)
    tri = jnp.tril(jnp.ones((CHUNK, CHUNK), dtype=bool))
    seg = a_cs[..., :, None] - a_cs[..., None, :]
    decay_ls = jnp.exp(jnp.where(tri, seg, -jnp.inf))
    cb = jnp.einsum("bclgn,bcsgn->bgcls", cm, bm)
    y_diag = jnp.einsum("bgcls,bghcls,bcsghp->bclghp", cb, decay_ls, xdt)
    decay_to_end = jnp.exp(a_cs[..., -1:] - a_cs)
    states = jnp.einsum("bclgn,bghcl,bclghp->bcghpn", bm, decay_to_end, xdt)
    chunk_decay = jnp.exp(a_cs[..., -1])

    def step(h, inp):
        s_c, d_c = inp
        return d_c[..., None, None] * h + s_c, h

    h0 = jnp.zeros((b, G, HG, P, SSD_STATE), states.dtype)
    _, prev = lax.scan(step, h0, (jnp.moveaxis(states, 1, 0), jnp.moveaxis(chunk_decay, -1, 0)))
    prev = jnp.moveaxis(prev, 0, 1)
    y_off = jnp.einsum("bclgn,bcghpn,bghcl->bclghp", cm, prev, jnp.exp(a_cs))
    return (y_diag + y_off).reshape(b, L, G, HG, P)


def ssd_mixer(h, valid, in_proj, conv_w, conv_b, dt_bias, a_log, d_skip, norm_w, out_proj):
    b, L, _ = h.shape
    proj = h @ in_proj
    z, xbc, dt = jnp.split(proj, [D_INNER, D_INNER + SSD_CONV_CH], axis=-1)
    vmask = valid.astype(h.dtype)[None, :, None]
    xbc = jax.nn.silu(depthwise_conv_centred(xbc * vmask, conv_w, conv_b)) * vmask
    xs, bm, cm = jnp.split(xbc, [D_INNER, D_INNER + SSD_GROUPS * SSD_STATE], axis=-1)
    xs = xs.reshape(b, L, SSD_GROUPS, SSD_HPG, SSD_HEAD_DIM)
    bm = bm.reshape(b, L, SSD_GROUPS, SSD_STATE)
    cm = cm.reshape(b, L, SSD_GROUPS, SSD_STATE)
    dt = jax.nn.softplus(dt.astype(jnp.float32).reshape(b, L, 2, SSD_HEADS)
                         + dt_bias.astype(jnp.float32)) * valid.astype(jnp.float32)[None, :, None, None]
    a_neg = -jnp.exp(a_log.astype(jnp.float32))
    dt_f = dt[:, :, 0].reshape(b, L, SSD_GROUPS, SSD_HPG)
    dt_b = dt[:, :, 1].reshape(b, L, SSD_GROUPS, SSD_HPG)
    a_f = dt_f * a_neg[0].reshape(SSD_GROUPS, SSD_HPG)
    a_b = dt_b * a_neg[1].reshape(SSD_GROUPS, SSD_HPG)
    y_fwd = ssd_chunked(xs * dt_f[..., None], a_f, bm, cm)
    y_bwd = jnp.flip(ssd_chunked(jnp.flip(xs * dt_b[..., None], 1), jnp.flip(a_b, 1),
                                 jnp.flip(bm, 1), jnp.flip(cm, 1)), 1)
    y = y_fwd + y_bwd + d_skip.reshape(SSD_GROUPS, SSD_HPG)[..., None] * xs
    y = y.reshape(b, L, D_INNER).astype(h.dtype)
    g = (y * jax.nn.silu(z)).reshape(b, L, SSD_GROUPS, D_INNER // SSD_GROUPS)
    g = rms_norm(g, norm_w.reshape(SSD_GROUPS, D_INNER // SSD_GROUPS)).reshape(b, L, D_INNER)
    return g @ out_proj


def axial_rope_tables(row, col):
    inv_freq = ROPE_THETA ** (-jnp.arange(0, ROPE_AXIS_DIM, 2, dtype=jnp.float32) / ROPE_AXIS_DIM)
    ang = jnp.stack([row, col], axis=-1).astype(jnp.float32)[..., None] * inv_freq
    return jnp.cos(ang), jnp.sin(ang)


def apply_axial_rope(x, cos, sin):
    xs = x.astype(jnp.float32).reshape(*x.shape[:-1], 2, 2, ROPE_HALF)
    x1, x2 = xs[..., 0, :], xs[..., 1, :]
    c, s = cos[:, None], sin[:, None]
    out = jnp.stack([x1 * c - x2 * s, x2 * c + x1 * s], axis=-2)
    return out.reshape(x.shape).astype(x.dtype)


def attention_mixer(h, cos, sin, valid, w_qkv, q_norm, k_norm, w_o):
    b, L, _ = h.shape
    qkv = h @ w_qkv
    q, k, v = jnp.split(qkv, [N_HEADS * HEAD_DIM, (N_HEADS + N_KV_HEADS) * HEAD_DIM], axis=-1)
    q = apply_axial_rope(rms_norm(q.reshape(b, L, N_HEADS, HEAD_DIM), q_norm), cos, sin)
    k = apply_axial_rope(rms_norm(k.reshape(b, L, N_KV_HEADS, HEAD_DIM), k_norm), cos, sin)
    v = v.reshape(b, L, N_KV_HEADS, HEAD_DIM)
    nb = L // Q_BLOCK
    qb = q.reshape(b, nb, Q_BLOCK, N_KV_HEADS, GQ, HEAD_DIM).transpose(1, 0, 2, 3, 4, 5)
    key_bias = jnp.where(valid, 0.0, -jnp.inf).astype(jnp.float32)
    scale = HEAD_DIM ** -0.5

    def block(q_blk):
        s = jnp.einsum("bqkgd,bskd->bkgqs", q_blk, k).astype(jnp.float32) * scale + key_bias
        p = jax.nn.softmax(s, axis=-1).astype(v.dtype)
        return jnp.einsum("bkgqs,bskd->bqkgd", p, v)

    o = lax.map(block, qb)
    o = o.transpose(1, 0, 2, 3, 4, 5).reshape(b, L, N_HEADS * HEAD_DIM)
    return o @ w_o


def setup_inputs(seed: int = 0) -> dict:
    key = jax.random.key(seed)
    ks = jax.random.split(key, 20)
    f32 = jnp.float32
    n_ssd = (DEPTH + N_MIXERS - 1) // N_MIXERS
    n_attn = DEPTH // N_MIXERS

    def nrm(k, shape, scale):
        return jax.random.normal(k, shape, f32) * scale

    def gain(k, shape):
        return 1.0 + 0.02 * jax.random.normal(k, shape, f32)

    dt0 = jnp.exp(jax.random.uniform(ks[8], (n_ssd, 2, SSD_HEADS), f32,
                                     math.log(1e-3), math.log(1e-1)))
    return {
        "x": jax.random.normal(ks[0], (BATCH, SEQ, D_MODEL), f32),
        "meta_tokens": nrm(ks[1], (N_META, D_MODEL), 1.0),
        "ffn_norm": gain(ks[2], (DEPTH, 2, D_MODEL)),
        "ffn_w_in": nrm(ks[3], (DEPTH, 2, D_MODEL, 2 * D_FF), D_MODEL ** -0.5),
        "ffn_w_out": nrm(ks[4], (DEPTH, 2, D_FF, D_MODEL), D_FF ** -0.5),
        "mix_norm": gain(ks[5], (DEPTH, D_MODEL)),
        "ssd_in_proj": nrm(ks[6], (n_ssd, D_MODEL, SSD_PROJ), D_MODEL ** -0.5),
        "ssd_conv_w": nrm(ks[7], (n_ssd, SSD_CONV, SSD_CONV_CH), SSD_CONV ** -0.5),
        "ssd_conv_b": nrm(ks[9], (n_ssd, SSD_CONV_CH), 0.01),
        "ssd_dt_bias": dt0 + jnp.log(-jnp.expm1(-dt0)),
        "ssd_A_log": jnp.log(jax.random.uniform(ks[10], (n_ssd, 2, SSD_HEADS), f32, 1.0, 16.0)),
        "ssd_D": gain(ks[11], (n_ssd, SSD_HEADS)),
        "ssd_norm": gain(ks[12], (n_ssd, D_INNER)),
        "ssd_out_proj": nrm(ks[13], (n_ssd, D_INNER, D_MODEL), D_INNER ** -0.5),
        "attn_w_qkv": nrm(ks[14], (n_attn, D_MODEL, QKV_W), D_MODEL ** -0.5),
        "attn_q_norm": gain(ks[15], (n_attn, HEAD_DIM)),
        "attn_k_norm": gain(ks[16], (n_attn, HEAD_DIM)),
        "attn_w_o": nrm(ks[17], (n_attn, N_HEADS * HEAD_DIM, D_MODEL), (N_HEADS * HEAD_DIM) ** -0.5),
    }


def reference(x, meta_tokens, ffn_norm, ffn_w_in, ffn_w_out, mix_norm, ssd_in_proj, ssd_conv_w,
              ssd_conv_b, ssd_dt_bias, ssd_A_log, ssd_D, ssd_norm, ssd_out_proj, attn_w_qkv,
              attn_q_norm, attn_k_norm, attn_w_o):
    b, n_tok, _ = x.shape
    rows_n = n_tok // GRID_W
    L = N_PAD + N_META + n_tok
    valid = jnp.arange(L) >= N_PAD
    row = jnp.concatenate([jnp.zeros((N_PAD,), jnp.int32),
                           jnp.full((N_META,), -1, jnp.int32),
                           jnp.repeat(jnp.arange(rows_n, dtype=jnp.int32), GRID_W)])
    col = jnp.concatenate([jnp.zeros((N_PAD,), jnp.int32),
                           jnp.arange(N_META, dtype=jnp.int32),
                           jnp.tile(jnp.arange(GRID_W, dtype=jnp.int32), rows_n)])
    cos, sin = axial_rope_tables(row, col)
    h = jnp.concatenate([jnp.zeros((b, N_PAD, D_MODEL), x.dtype),
                         jnp.broadcast_to(meta_tokens.astype(x.dtype)[None], (b, N_META, D_MODEL)),
                         x], axis=1)
    for i in range(DEPTH):
        j = i // N_MIXERS
        h = h + 0.5 * swiglu_ffn(h, ffn_norm[i, 0], ffn_w_in[i, 0], ffn_w_out[i, 0])
        hn = rms_norm(h, mix_norm[i])
        if i % N_MIXERS == 0:
            h = h + ssd_mixer(hn, valid, ssd_in_proj[j], ssd_conv_w[j], ssd_conv_b[j], ssd_dt_bias[j],
                              ssd_A_log[j], ssd_D[j], ssd_norm[j], ssd_out_proj[j])
        else:
            h = h + attention_mixer(hn, cos, sin, valid, attn_w_qkv[j], attn_q_norm[j],
                                    attn_k_norm[j], attn_w_o[j])
        h = h + 0.5 * swiglu_ffn(h, ffn_norm[i, 1], ffn_w_in[i, 1], ffn_w_out[i, 1])
    return h[:, N_PAD + N_META:, :]
```

```python
import functools

import jax
import jax.numpy as jnp
from jax import lax
from jax.experimental import pallas as pl
from jax.experimental.pallas import tpu as pltpu

F32 = jnp.float32
BF16 = jnp.bfloat16

GRID_W = 64
CHUNK = 128
EPS = 1e-6
SSD_HEAD_DIM = 64
SSD_GROUPS = 8
SSD_STATE = 128
HEAD_DIM = 128
GQ = 2
ROPE_THETA = 10000.0
ROPE_HALF = HEAD_DIM // 4

V7X_VMEM_LIMIT_BYTES = 56 * 1024 * 1024
LANES = 128


def _params(*sem):
    return pltpu.CompilerParams(dimension_semantics=sem, vmem_limit_bytes=V7X_VMEM_LIMIT_BYTES)


def _pick(n, candidates):
    for c in candidates:
        if n % c == 0:
            return c
    return n


def _rms_rows(x, w):
    ms = jnp.mean(x * x, axis=-1, keepdims=True)
    return x * lax.rsqrt(ms + EPS) * w


def _silu(x):
    return x / (1.0 + jnp.exp(-x))


def _ffn_kernel(x_ref, nw_ref, wg_ref, wu_ref, wo_ref, o_ref, hn_ref):
    j = pl.program_id(1)

    @pl.when(j == 0)
    def _():
        x = x_ref[...]
        hn_ref[...] = _rms_rows(x, nw_ref[...]).astype(BF16)
        o_ref[...] = x

    hn = hn_ref[...]
    g = jnp.dot(hn, wg_ref[...], preferred_element_type=F32)
    u = jnp.dot(hn, wu_ref[...], preferred_element_type=F32)
    a = (0.5 * _silu(g) * u).astype(BF16)
    o_ref[...] += jnp.dot(a, wo_ref[...], preferred_element_type=F32)


def _ffn(h, norm_w, w_in, w_out):
    m, d = h.shape
    f = w_out.shape[0]
    tm = _pick(m, (768, 384, 256, 128))
    tf = _pick(f, (512, 256, 128))
    nf = f // tf
    return pl.pallas_call(
        _ffn_kernel,
        grid=(m // tm, nf),
        in_specs=[
            pl.BlockSpec((tm, d), lambda i, j: (i, 0)),
            pl.BlockSpec((1, d), lambda i, j: (0, 0)),
            pl.BlockSpec((d, tf), lambda i, j: (0, j)),
            pl.BlockSpec((d, tf), lambda i, j: (0, j + nf)),
            pl.BlockSpec((tf, d), lambda i, j: (j, 0)),
        ],
        out_specs=pl.BlockSpec((tm, d), lambda i, j: (i, 0)),
        out_shape=jax.ShapeDtypeStruct((m, d), F32),
        scratch_shapes=[pltpu.VMEM((tm, d), BF16)],
        compiler_params=_params("parallel", "arbitrary"),
        name="ffn",
    )(h, norm_w.reshape(1, d), w_in, w_in, w_out)


def _norm_matmul_kernel(x_ref, nw_ref, w_ref, o_ref, hn_ref):
    @pl.when(pl.program_id(1) == 0)
    def _():
        hn_ref[...] = _rms_rows(x_ref[...], nw_ref[...]).astype(BF16)

    o_ref[...] = jnp.dot(hn_ref[...], w_ref[...], preferred_element_type=F32).astype(o_ref.dtype)


def _norm_matmul(h, norm_w, w, out_dtype, name):
    m, d = h.shape
    n = w.shape[1]
    tm = _pick(m, (768, 384, 256, 128))
    tn = _pick(n, (1024, 512, 256, 128))
    return pl.pallas_call(
        _norm_matmul_kernel,
        grid=(m // tm, n // tn),
        in_specs=[
            pl.BlockSpec((tm, d), lambda i, j: (i, 0)),
            pl.BlockSpec((1, d), lambda i, j: (0, 0)),
            pl.BlockSpec((d, tn), lambda i, j: (0, j)),
        ],
        out_specs=pl.BlockSpec((tm, tn), lambda i, j: (i, j)),
        out_shape=jax.ShapeDtypeStruct((m, n), out_dtype),
        scratch_shapes=[pltpu.VMEM((tm, d), BF16)],
        compiler_params=_params("parallel", "arbitrary"),
        name=name,
    )(h, norm_w.reshape(1, d), w)


def _matmul_res_kernel(a_ref, w_ref, r_ref, o_ref):
    o_ref[...] = r_ref[...] + jnp.dot(a_ref[...], w_ref[...], preferred_element_type=F32)


def _matmul_res(a, w, res, name):
    m, k = a.shape
    n = w.shape[1]
    tm = _pick(m, (768, 384, 256, 128))
    tn = _pick(n, (1024, 512, 256, 128))
    return pl.pallas_call(
        _matmul_res_kernel,
        grid=(m // tm, n // tn),
        in_specs=[
            pl.BlockSpec((tm, k), lambda i, j: (i, 0)),
            pl.BlockSpec((k, tn), lambda i, j: (0, j)),
            pl.BlockSpec((tm, tn), lambda i, j: (i, j)),
        ],
        out_specs=pl.BlockSpec((tm, tn), lambda i, j: (i, j)),
        out_shape=jax.ShapeDtypeStruct((m, n), F32),
        compiler_params=_params("parallel", "parallel"),
        name=name,
    )(a, w, res)


def _qkv_kernel(x_ref, nw_ref, w_ref, hw_ref, cos_ref, sin_ref, o_ref, hn_ref, *, n_rope_tiles):
    j = pl.program_id(1)

    @pl.when(j == 0)
    def _():
        hn_ref[...] = _rms_rows(x_ref[...], nw_ref[...]).astype(BF16)

    acc = jnp.dot(hn_ref[...], w_ref[...], preferred_element_type=F32)
    tn = acc.shape[1]

    @pl.when(j < n_rope_tiles)
    def _():
        c = cos_ref[...]
        s = sin_ref[...]
        lane = lax.broadcasted_iota(jnp.int32, (1, HEAD_DIM), 1)
        first_half = (lane % (2 * ROPE_HALF)) < ROPE_HALF
        for hh in range(tn // HEAD_DIM):
            sl = slice(hh * HEAD_DIM, (hh + 1) * HEAD_DIM)
            yn = _rms_rows(acc[:, sl], hw_ref[:, sl])
            partner = jnp.where(first_half,
                                pltpu.roll(yn, HEAD_DIM - ROPE_HALF, 1),
                                pltpu.roll(yn, ROPE_HALF, 1))
            o_ref[:, sl] = (yn * c + partner * s).astype(o_ref.dtype)

    @pl.when(j >= n_rope_tiles)
    def _():
        o_ref[...] = acc.astype(o_ref.dtype)


def _qkv_proj(h, norm_w, w, head_w, cos, sin, seq_len, n_rope_cols):
    m, d = h.shape
    n = w.shape[1]
    tm = _pick(seq_len, (1408, 384, 128))
    tn = _pick(n_rope_cols, (512, 256, 128))
    assert n % tn == 0
    tiles_per_seq = seq_len // tm
    kern = functools.partial(_qkv_kernel, n_rope_tiles=n_rope_cols // tn)
    return pl.pallas_call(
        kern,
        grid=(m // tm, n // tn),
        in_specs=[
            pl.BlockSpec((tm, d), lambda i, j: (i, 0)),
            pl.BlockSpec((1, d), lambda i, j: (0, 0)),
            pl.BlockSpec((d, tn), lambda i, j: (0, j)),
            pl.BlockSpec((1, tn), lambda i, j: (0, j)),
            pl.BlockSpec((tm, HEAD_DIM), lambda i, j: (i % tiles_per_seq, 0)),
            pl.BlockSpec((tm, HEAD_DIM), lambda i, j: (i % tiles_per_seq, 0)),
        ],
        out_specs=pl.BlockSpec((tm, tn), lambda i, j: (i, j)),
        out_shape=jax.ShapeDtypeStruct((m, n), BF16),
        scratch_shapes=[pltpu.VMEM((tm, d), BF16)],
        compiler_params=_params("parallel", "arbitrary"),
        name="attn_qkv",
    )(h, norm_w.reshape(1, d), w, head_w, cos, sin)


def _attn_kernel(q_ref, k_ref, v_ref, o_ref, *, n_pad):
    k = k_ref[0]
    v = v_ref[0]
    seq = k.shape[0]
    key_valid = lax.broadcasted_iota(jnp.int32, (1, seq), 1) >= n_pad
    for g in range(GQ):
        sl = slice(g * HEAD_DIM, (g + 1) * HEAD_DIM)
        q = q_ref[0, :, sl]
        s = lax.dot_general(q, k, (((1,), (1,)), ((), ())), preferred_element_type=F32)
        s = jnp.where(key_valid, s, -jnp.inf)
        p = jnp.exp(s - jnp.max(s, axis=-1, keepdims=True))
        denom = jnp.sum(p, axis=-1, keepdims=True)
        o = jnp.dot(p.astype(BF16), v, preferred_element_type=F32)
        o_ref[0, :, sl] = (o / denom).astype(o_ref.dtype)


def _attention(qkv, batch, seq_len, n_heads, n_pad):
    n_kv = n_heads // GQ
    qkv3 = qkv.reshape(batch, seq_len, (n_heads + 2 * n_kv) * HEAD_DIM)
    tq = _pick(seq_len, (384, 128))
    qw = GQ * HEAD_DIM
    k_off = n_heads
    v_off = n_heads + n_kv
    return pl.pallas_call(
        functools.partial(_attn_kernel, n_pad=n_pad),
        grid=(batch, n_kv, seq_len // tq),
        in_specs=[
            pl.BlockSpec((1, tq, qw), lambda b, h, i: (b, i, h)),
            pl.BlockSpec((1, seq_len, HEAD_DIM), lambda b, h, i: (b, 0, k_off + h)),
            pl.BlockSpec((1, seq_len, HEAD_DIM), lambda b, h, i: (b, 0, v_off + h)),
        ],
        out_specs=pl.BlockSpec((1, tq, qw), lambda b, h, i: (b, i, h)),
        out_shape=jax.ShapeDtypeStruct((batch, seq_len, n_heads * HEAD_DIM), BF16),
        compiler_params=_params("parallel", "parallel", "arbitrary"),
        name="attn_core",
    )(qkv3, qkv3, qkv3)


def _conv_kernel(u_ref, w_ref, b_ref, o_ref, pad_ref, *, n_pad, halo):
    seq = u_ref.shape[1]
    tc = u_ref.shape[2]
    taps = w_ref.shape[0]
    half = (taps - 1) // 2
    nc = seq // CHUNK
    zeros = jnp.zeros((halo, tc), F32)
    pad_ref[pl.ds(0, halo), :] = zeros
    pad_ref[pl.ds(halo + seq, halo), :] = zeros

    def valid_rows(c):
        rows = c * CHUNK + lax.broadcasted_iota(jnp.int32, (CHUNK, 1), 0)
        return (rows >= n_pad).astype(F32)

    def fill(c, carry):
        r0 = pl.multiple_of(c * CHUNK, CHUNK)
        pad_ref[pl.ds(halo + r0, CHUNK), :] = u_ref[0, pl.ds(r0, CHUNK), :] * valid_rows(c)
        return carry

    lax.fori_loop(0, nc, fill, 0)

    def conv(c, carry):
        r0 = pl.multiple_of(c * CHUNK, CHUNK)
        window = pad_ref[pl.ds(r0, CHUNK + 2 * halo), :]
        acc = jnp.broadcast_to(b_ref[...], (CHUNK, tc))
        for t in range(taps):
            off = halo - half + t
            acc = acc + w_ref[pl.ds(t, 1), :] * window[off:off + CHUNK, :]
        o_ref[0, pl.ds(r0, CHUNK), :] = (_silu(acc) * valid_rows(c)).astype(o_ref.dtype)
        return carry

    lax.fori_loop(0, nc, conv, 0)


def _ssd_conv(zx3, conv_w, conv_b, d_inner, n_pad):
    batch, seq_len, _ = zx3.shape
    taps, ch = conv_w.shape
    tc = _pick(ch, (512, 256, 128))
    assert d_inner % tc == 0
    col0 = d_inner // tc
    halo = 8
    return pl.pallas_call(
        functools.partial(_conv_kernel, n_pad=n_pad, halo=halo),
        grid=(batch, ch // tc),
        in_specs=[
            pl.BlockSpec((1, seq_len, tc), lambda b, j: (b, 0, col0 + j)),
            pl.BlockSpec((taps, tc), lambda b, j: (0, j)),
            pl.BlockSpec((1, tc), lambda b, j: (0, j)),
        ],
        out_specs=pl.BlockSpec((1, seq_len, tc), lambda b, j: (b, 0, j)),
        out_shape=jax.ShapeDtypeStruct((batch, seq_len, ch), BF16),
        scratch_shapes=[pltpu.VMEM((seq_len + 2 * halo, tc), F32)],
        compiler_params=_params("parallel", "parallel"),
        name="ssd_conv",
    )(zx3, conv_w, conv_b.reshape(1, ch))


def _softplus(x):
    return jnp.maximum(x, 0.0) + jnp.log(1.0 + jnp.exp(-jnp.abs(x)))


def _ssd_kernel(xs_ref, b_ref, c_ref, dtc_ref, dtr_ref, pr_ref, pc_ref, dskip_ref, y_ref, st_ref,
                *, n_pad, hpg):
    seq = xs_ref.shape[1]
    nc = seq // CHUNK
    p = SSD_HEAD_DIM
    hi = lax.Precision.HIGHEST

    row_i = lax.broadcasted_iota(jnp.int32, (CHUNK, CHUNK), 0)
    col_i = lax.broadcasted_iota(jnp.int32, (CHUNK, CHUNK), 1)
    lower = row_i >= col_i
    upper = row_i <= col_i
    lower_f = lower.astype(F32)
    upper_f = upper.astype(F32)

    bias_row = pr_ref[0, pl.ds(0, 1), :]
    aneg_row = -jnp.exp(pr_ref[0, pl.ds(1, 1), :])
    bias_col = pc_ref[0, :, pl.ds(0, 1)]
    aneg_col = -jnp.exp(pc_ref[0, :, pl.ds(1, 1)])
    dskip = dskip_ref[0]

    def chunk_inputs(c):
        r0 = pl.multiple_of(c * CHUNK, CHUNK)
        rows = c * CHUNK + lax.broadcasted_iota(jnp.int32, (CHUNK, 1), 0)
        lanes = c * CHUNK + lax.broadcasted_iota(jnp.int32, (1, CHUNK), 1)
        dt_col = _softplus(dtc_ref[0, 0, c] + bias_row) * (rows >= n_pad).astype(F32)
        dt_row = _softplus(dtr_ref[0, 0, c] + bias_col) * (lanes >= n_pad).astype(F32)
        x = xs_ref[0, pl.ds(r0, CHUNK), :]
        bm = b_ref[0, pl.ds(r0, CHUNK), :]
        cm = c_ref[0, pl.ds(r0, CHUNK), :]
        cb = lax.dot_general(cm, bm, (((1,), (1,)), ((), ())), preferred_element_type=F32)
        bt = bm.astype(F32).T
        return r0, dt_col, dt_row, x, cm, cb, bt

    def scan_chunk(c, *, backward):
        r0, dt_col, dt_row, x, cm, cb, bt = chunk_inputs(c)
        a_col = dt_col * aneg_row
        a_row = dt_row * aneg_col
        if backward:
            cs_col = jnp.dot(upper_f, a_col, precision=hi, preferred_element_type=F32)
            cs_row = jnp.dot(a_row, lower_f, precision=hi, preferred_element_type=F32)
            tri = upper
            end = 0
        else:
            cs_col = jnp.dot(lower_f, a_col, precision=hi, preferred_element_type=F32)
            cs_row = jnp.dot(a_row, upper_f, precision=hi, preferred_element_type=F32)
            tri = lower
            end = CHUNK - 1
        cm_f = cm.astype(F32)
        outs = []
        for j in range(hpg):
            hh = hpg + j if backward else j
            xh = x[:, j * p:(j + 1) * p]
            csc = cs_col[:, hh:hh + 1]
            csr = cs_row[hh:hh + 1, :]
            dtr = dt_row[hh:hh + 1, :]
            seg = jnp.where(tri, csc - csr, -jnp.inf)
            m_diag = cb * jnp.exp(seg) * dtr
            c_scaled = cm_f * jnp.exp(csc)
            state = st_ref[j]
            lhs = jnp.concatenate([m_diag, c_scaled], axis=1).astype(BF16)
            rhs = jnp.concatenate([xh, state.astype(BF16)], axis=0)
            outs.append(jnp.dot(lhs, rhs, preferred_element_type=F32))
            total = cs_row[hh:hh + 1, end:end + 1]
            w_row = jnp.exp(total - csr) * dtr
            upd = jnp.dot((bt * w_row).astype(BF16), xh, preferred_element_type=F32)
            st_ref[j] = jnp.exp(total) * state + upd
        y = jnp.concatenate(outs, axis=1)
        if backward:
            y_ref[0, pl.ds(r0, CHUNK), :] += y
        else:
            y_ref[0, pl.ds(r0, CHUNK), :] = y + dskip * x.astype(F32)

    def run(backward):
        st_ref[...] = jnp.zeros(st_ref.shape, F32)

        def body(i, carry):
            scan_chunk(nc - 1 - i if backward else i, backward=backward)
            return carry

        lax.fori_loop(0, nc, body, 0)

    run(False)
    run(True)


def _ssd_core(xbc, dt_raw, dt_bias, a_log, d_skip, batch, seq_len, d_inner, n_pad):
    groups = SSD_GROUPS
    heads = d_inner // SSD_HEAD_DIM
    hpg = heads // groups
    gw = hpg * SSD_HEAD_DIM
    nc = seq_len // CHUNK
    assert gw % LANES == 0
    b_off = d_inner // SSD_STATE
    c_off = b_off + groups
    dt5 = dt_raw.reshape(batch, nc, CHUNK, 2, groups, hpg).transpose(0, 4, 1, 2, 3, 5)
    dt_colform = dt5.reshape(batch, groups, nc, CHUNK, 2 * hpg)
    dt_rowform = dt_colform.transpose(0, 1, 2, 4, 3)
    per_head = jnp.stack([dt_bias.reshape(2, groups, hpg), a_log.reshape(2, groups, hpg)], 0)
    p_row = per_head.transpose(2, 0, 1, 3).reshape(groups, 2, 2 * hpg).astype(F32)
    p_col = p_row.transpose(0, 2, 1)
    d_exp = jnp.repeat(d_skip.astype(F32), SSD_HEAD_DIM).reshape(groups, 1, gw)
    return pl.pallas_call(
        functools.partial(_ssd_kernel, n_pad=n_pad, hpg=hpg),
        grid=(batch, groups),
        in_specs=[
            pl.BlockSpec((1, seq_len, gw), lambda b, g: (b, 0, g)),
            pl.BlockSpec((1, seq_len, SSD_STATE), lambda b, g: (b, 0, b_off + g)),
            pl.BlockSpec((1, seq_len, SSD_STATE), lambda b, g: (b, 0, c_off + g)),
            pl.BlockSpec((1, 1, nc, CHUNK, 2 * hpg), lambda b, g: (b, g, 0, 0, 0)),
            pl.BlockSpec((1, 1, nc, 2 * hpg, CHUNK), lambda b, g: (b, g, 0, 0, 0)),
            pl.BlockSpec((1, 2, 2 * hpg), lambda b, g: (g, 0, 0)),
            pl.BlockSpec((1, 2 * hpg, 2), lambda b, g: (g, 0, 0)),
            pl.BlockSpec((1, 1, gw), lambda b, g: (g, 0, 0)),
        ],
        out_specs=pl.BlockSpec((1, seq_len, gw), lambda b, g: (b, 0, g)),
        out_shape=jax.ShapeDtypeStruct((batch, seq_len, d_inner), F32),
        scratch_shapes=[pltpu.VMEM((hpg, SSD_STATE, SSD_HEAD_DIM), F32)],
        compiler_params=_params("parallel", "parallel"),
        name="ssd_core",
    )(xbc, xbc, xbc, dt_colform, dt_rowform, p_row, p_col, d_exp)


def _gate_out_kernel(y_ref, z_ref, nw_ref, w_ref, r_ref, o_ref, g_ref, *, group_w):
    @pl.when(pl.program_id(1) == 0)
    def _():
        for gi in range(y_ref.shape[1] // group_w):
            sl = slice(gi * group_w, (gi + 1) * group_w)
            gated = y_ref[:, sl] * _silu(z_ref[:, sl])
            g_ref[:, sl] = _rms_rows(gated, nw_ref[:, sl]).astype(BF16)

    o_ref[...] = r_ref[...] + jnp.dot(g_ref[...], w_ref[...], preferred_element_type=F32)


def _gate_out(y, zx, norm_w, w, res, d_inner):
    m = y.shape[0]
    n = w.shape[1]
    tm = _pick(m, (384, 256, 128))
    tn = _pick(n, (512, 256, 128))
    return pl.pallas_call(
        functools.partial(_gate_out_kernel, group_w=d_inner // SSD_GROUPS),
        grid=(m // tm, n // tn),
        in_specs=[
            pl.BlockSpec((tm, d_inner), lambda i, j: (i, 0)),
            pl.BlockSpec((tm, d_inner), lambda i, j: (i, 0)),
            pl.BlockSpec((1, d_inner), lambda i, j: (0, 0)),
            pl.BlockSpec((d_inner, tn), lambda i, j: (0, j)),
            pl.BlockSpec((tm, tn), lambda i, j: (i, j)),
        ],
        out_specs=pl.BlockSpec((tm, tn), lambda i, j: (i, j)),
        out_shape=jax.ShapeDtypeStruct((m, n), F32),
        scratch_shapes=[pltpu.VMEM((tm, d_inner), BF16)],
        compiler_params=_params("parallel", "arbitrary"),
        name="ssd_gate_out",
    )(y, zx, norm_w.reshape(1, d_inner), w, res)


def _ssd_mixer(h, batch, seq_len, n_pad, mix_norm, in_proj, conv_w, conv_b, dt_bias, a_log, d_skip,
               norm_w, out_proj):
    d_inner = out_proj.shape[0]
    heads = d_inner // SSD_HEAD_DIM
    n_zx = in_proj.shape[1] - 2 * heads
    zx = _norm_matmul(h, mix_norm, in_proj[:, :n_zx].astype(BF16), F32, "ssd_in_zx")
    dt_raw = _norm_matmul(h, mix_norm, in_proj[:, n_zx:].astype(BF16), F32, "ssd_in_dt")
    xbc = _ssd_conv(zx.reshape(batch, seq_len, n_zx), conv_w, conv_b, d_inner, n_pad)
    y = _ssd_core(xbc, dt_raw, dt_bias, a_log, d_skip, batch, seq_len, d_inner, n_pad)
    return _gate_out(y.reshape(batch * seq_len, d_inner), zx, norm_w, out_proj.astype(BF16), h, d_inner)


def _rope_tables(seq_len, n_pad, n_meta):
    n_tok = seq_len - n_pad - n_meta
    rows_n = n_tok // GRID_W
    row = jnp.concatenate([jnp.zeros((n_pad,), jnp.int32), jnp.full((n_meta,), -1, jnp.int32),
                           jnp.repeat(jnp.arange(rows_n, dtype=jnp.int32), GRID_W)])
    col = jnp.concatenate([jnp.zeros((n_pad,), jnp.int32), jnp.arange(n_meta, dtype=jnp.int32),
                           jnp.tile(jnp.arange(GRID_W, dtype=jnp.int32), rows_n)])
    inv_freq = ROPE_THETA ** (-jnp.arange(0, 2 * ROPE_HALF, 2, dtype=F32) / (2 * ROPE_HALF))
    ang_r = row.astype(F32)[:, None] * inv_freq
    ang_c = col.astype(F32)[:, None] * inv_freq
    cos = jnp.concatenate([jnp.cos(ang_r), jnp.cos(ang_r), jnp.cos(ang_c), jnp.cos(ang_c)], -1)
    sin = jnp.concatenate([-jnp.sin(ang_r), jnp.sin(ang_r), -jnp.sin(ang_c), jnp.sin(ang_c)], -1)
    return cos, sin


def _attention_mixer(h, batch, seq_len, n_pad, cos, sin, mix_norm, w_qkv, q_norm, k_norm, w_o):
    n_heads = w_o.shape[0] // HEAD_DIM
    n_kv = n_heads // GQ
    head_w = jnp.concatenate([jnp.tile(q_norm.astype(F32) * HEAD_DIM ** -0.5, n_heads),
                              jnp.tile(k_norm.astype(F32), n_kv),
                              jnp.ones((n_kv * HEAD_DIM,), F32)]).reshape(1, -1)
    qkv = _qkv_proj(h, mix_norm, w_qkv.astype(BF16), head_w, cos, sin, seq_len,
                    (n_heads + n_kv) * HEAD_DIM)
    o = _attention(qkv, batch, seq_len, n_heads, n_pad)
    return _matmul_res(o.reshape(batch * seq_len, n_heads * HEAD_DIM), w_o.astype(BF16), h, "attn_out")


def kernel(x, meta_tokens, ffn_norm, ffn_w_in, ffn_w_out, mix_norm, ssd_in_proj, ssd_conv_w, ssd_conv_b,
           ssd_dt_bias, ssd_A_log, ssd_D, ssd_norm, ssd_out_proj, attn_w_qkv, attn_q_norm, attn_k_norm,
           attn_w_o):
    batch, n_tok, d = x.shape
    n_meta = meta_tokens.shape[0]
    n_pad = CHUNK - n_meta
    seq_len = n_pad + n_meta + n_tok
    depth = ffn_norm.shape[0]
    cos, sin = _rope_tables(seq_len, n_pad, n_meta)
    h = jnp.concatenate([jnp.zeros((batch, n_pad, d), x.dtype),
                         jnp.broadcast_to(meta_tokens.astype(x.dtype)[None], (batch, n_meta, d)),
                         x], axis=1).reshape(batch * seq_len, d)
    for i in range(depth):
        j = i // 2
        h = _ffn(h, ffn_norm[i, 0], ffn_w_in[i, 0].astype(BF16), ffn_w_out[i, 0].astype(BF16))
        if i % 2 == 0:
            h = _ssd_mixer(h, batch, seq_len, n_pad, mix_norm[i], ssd_in_proj[j], ssd_conv_w[j],
                           ssd_conv_b[j], ssd_dt_bias[j], ssd_A_log[j], ssd_D[j], ssd_norm[j],
                           ssd_out_proj[j])
        else:
            h = _attention_mixer(h, batch, seq_len, n_pad, cos, sin, mix_norm[i], attn_w_qkv[j],
                                 attn_q_norm[j], attn_k_norm[j], attn_w_o[j])
        h = _ffn(h, ffn_norm[i, 1], ffn_w_in[i, 1].astype(BF16), ffn_w_out[i, 1].astype(BF16))
    return h.reshape(batch, seq_len, d)[:, n_pad + n_meta:, :]
```

```python
import functools

import jax
import jax.numpy as jnp
from jax import lax
from jax.experimental import pallas as pl
from jax.experimental.pallas import tpu as pltpu

F32 = jnp.float32
BF16 = jnp.bfloat16

GRID_W = 64
CHUNK = 128
EPS = 1e-6
SSD_HEAD_DIM = 64
SSD_GROUPS = 8
SSD_STATE = 128
HEAD_DIM = 128
GQ = 2
ROPE_THETA = 10000.0
ROPE_HALF = HEAD_DIM // 4
LOG2E = 1.4426950408889634

V7X_VMEM_LIMIT_BYTES = 56 * 1024 * 1024
LANES = 128


def _params(*sem):
    return pltpu.CompilerParams(dimension_semantics=sem, vmem_limit_bytes=V7X_VMEM_LIMIT_BYTES)


def _pick(n, candidates):
    for c in candidates:
        if n % c == 0:
            return c
    return n


def _rms_rows(x, w):
    ms = jnp.mean(x * x, axis=-1, keepdims=True)
    return x * lax.rsqrt(ms + EPS) * w


def _silu(x):
    return x / (1.0 + jnp.exp(-x))


def _ffn_kernel(x_ref, nw_ref, wg_ref, wu_ref, wo_ref, o_ref, hn_ref):
    j = pl.program_id(1)

    @pl.when(j == 0)
    def _():
        x = x_ref[...]
        hn_ref[...] = _rms_rows(x, nw_ref[...]).astype(BF16)
        o_ref[...] = x

    hn = hn_ref[...]
    g = jnp.dot(hn, wg_ref[...], preferred_element_type=F32)
    u = jnp.dot(hn, wu_ref[...], preferred_element_type=F32)
    a = (0.5 * _silu(g) * u).astype(BF16)
    o_ref[...] += jnp.dot(a, wo_ref[...], preferred_element_type=F32)


def _ffn(h, norm_w, w_in, w_out):
    m, d = h.shape
    f = w_out.shape[0]
    tm = _pick(m, (768, 384, 256, 128))
    tf = _pick(f, (512, 256, 128))
    nf = f // tf
    return pl.pallas_call(
        _ffn_kernel,
        grid=(m // tm, nf),
        in_specs=[
            pl.BlockSpec((tm, d), lambda i, j: (i, 0)),
            pl.BlockSpec((1, d), lambda i, j: (0, 0)),
            pl.BlockSpec((d, tf), lambda i, j: (0, j)),
            pl.BlockSpec((d, tf), lambda i, j: (0, j + nf)),
            pl.BlockSpec((tf, d), lambda i, j: (j, 0)),
        ],
        out_specs=pl.BlockSpec((tm, d), lambda i, j: (i, 0)),
        out_shape=jax.ShapeDtypeStruct((m, d), F32),
        scratch_shapes=[pltpu.VMEM((tm, d), BF16)],
        compiler_params=_params("parallel", "arbitrary"),
        name="ffn",
    )(h, norm_w.reshape(1, d), w_in, w_in, w_out)


def _norm_matmul_kernel(x_ref, nw_ref, w_ref, o_ref, hn_ref):
    @pl.when(pl.program_id(1) == 0)
    def _():
        hn_ref[...] = _rms_rows(x_ref[...], nw_ref[...]).astype(BF16)

    o_ref[...] = jnp.dot(hn_ref[...], w_ref[...], preferred_element_type=F32).astype(o_ref.dtype)


def _norm_matmul(h, norm_w, w, out_dtype, name):
    m, d = h.shape
    n = w.shape[1]
    tm = _pick(m, (768, 384, 256, 128))
    tn = _pick(n, (1024, 512, 256, 128))
    return pl.pallas_call(
        _norm_matmul_kernel,
        grid=(m // tm, n // tn),
        in_specs=[
            pl.BlockSpec((tm, d), lambda i, j: (i, 0)),
            pl.BlockSpec((1, d), lambda i, j: (0, 0)),
            pl.BlockSpec((d, tn), lambda i, j: (0, j)),
        ],
        out_specs=pl.BlockSpec((tm, tn), lambda i, j: (i, j)),
        out_shape=jax.ShapeDtypeStruct((m, n), out_dtype),
        scratch_shapes=[pltpu.VMEM((tm, d), BF16)],
        compiler_params=_params("parallel", "arbitrary"),
        name=name,
    )(h, norm_w.reshape(1, d), w)


def _matmul_res_kernel(a_ref, w_ref, r_ref, o_ref):
    o_ref[...] = r_ref[...] + jnp.dot(a_ref[...], w_ref[...], preferred_element_type=F32)


def _matmul_res(a, w, res, name):
    m, k = a.shape
    n = w.shape[1]
    tm = _pick(m, (768, 384, 256, 128))
    tn = _pick(n, (1024, 512, 256, 128))
    return pl.pallas_call(
        _matmul_res_kernel,
        grid=(m // tm, n // tn),
        in_specs=[
            pl.BlockSpec((tm, k), lambda i, j: (i, 0)),
            pl.BlockSpec((k, tn), lambda i, j: (0, j)),
            pl.BlockSpec((tm, tn), lambda i, j: (i, j)),
        ],
        out_specs=pl.BlockSpec((tm, tn), lambda i, j: (i, j)),
        out_shape=jax.ShapeDtypeStruct((m, n), F32),
        compiler_params=_params("parallel", "parallel"),
        name=name,
    )(a, w, res)


def _qkv_kernel(x_ref, nw_ref, w_ref, hw_ref, cos_ref, sin_ref, o_ref, hn_ref, *, n_rope_tiles):
    j = pl.program_id(1)

    @pl.when(j == 0)
    def _():
        hn_ref[...] = _rms_rows(x_ref[...], nw_ref[...]).astype(BF16)

    acc = jnp.dot(hn_ref[...], w_ref[...], preferred_element_type=F32)
    tn = acc.shape[1]

    @pl.when(j < n_rope_tiles)
    def _():
        c = cos_ref[...]
        s = sin_ref[...]
        lane = lax.broadcasted_iota(jnp.int32, (1, HEAD_DIM), 1)
        first_half = (lane % (2 * ROPE_HALF)) < ROPE_HALF
        for hh in range(tn // HEAD_DIM):
            sl = slice(hh * HEAD_DIM, (hh + 1) * HEAD_DIM)
            yn = _rms_rows(acc[:, sl], hw_ref[:, sl])
            partner = jnp.where(first_half,
                                pltpu.roll(yn, HEAD_DIM - ROPE_HALF, 1),
                                pltpu.roll(yn, ROPE_HALF, 1))
            o_ref[:, sl] = (yn * c + partner * s).astype(o_ref.dtype)

    @pl.when(j >= n_rope_tiles)
    def _():
        o_ref[...] = acc.astype(o_ref.dtype)


def _qkv_proj(h, norm_w, w, head_w, cos, sin, seq_len, n_rope_cols):
    m, d = h.shape
    n = w.shape[1]
    tm = _pick(seq_len, (1408, 384, 128))
    tn = _pick(n_rope_cols, (512, 256, 128))
    assert n % tn == 0
    tiles_per_seq = seq_len // tm
    kern = functools.partial(_qkv_kernel, n_rope_tiles=n_rope_cols // tn)
    return pl.pallas_call(
        kern,
        grid=(m // tm, n // tn),
        in_specs=[
            pl.BlockSpec((tm, d), lambda i, j: (i, 0)),
            pl.BlockSpec((1, d), lambda i, j: (0, 0)),
            pl.BlockSpec((d, tn), lambda i, j: (0, j)),
            pl.BlockSpec((1, tn), lambda i, j: (0, j)),
            pl.BlockSpec((tm, HEAD_DIM), lambda i, j: (i % tiles_per_seq, 0)),
            pl.BlockSpec((tm, HEAD_DIM), lambda i, j: (i % tiles_per_seq, 0)),
        ],
        out_specs=pl.BlockSpec((tm, tn), lambda i, j: (i, j)),
        out_shape=jax.ShapeDtypeStruct((m, n), BF16),
        scratch_shapes=[pltpu.VMEM((tm, d), BF16)],
        compiler_params=_params("parallel", "arbitrary"),
        name="attn_qkv",
    )(h, norm_w.reshape(1, d), w, head_w, cos, sin)


def _attn_kernel(q_ref, kt_ref, v_ref, o_ref, *, n_pad, n_sub):
    kt = kt_ref[0, 0]
    v = v_ref[0]
    rs = q_ref.shape[1] // n_sub
    key_valid = lax.broadcasted_iota(jnp.int32, (1, LANES), 1) >= n_pad
    assert n_pad <= LANES
    units = [(slice(r * rs, (r + 1) * rs), slice(g * HEAD_DIM, (g + 1) * HEAD_DIM))
             for g in range(GQ) for r in range(n_sub)]

    def scores(unit):
        rows, cols = unit
        return jnp.dot(q_ref[0, rows, cols], kt, preferred_element_type=F32)

    def finish(unit, s):
        rows, cols = unit
        s = jnp.concatenate([jnp.where(key_valid, s[:, :LANES], -jnp.inf), s[:, LANES:]], axis=1)
        p = jnp.exp2(s - jnp.max(s, axis=-1, keepdims=True))
        denom = jnp.sum(p, axis=-1, keepdims=True)
        o = jnp.dot(p.astype(BF16), v, preferred_element_type=F32)
        o_ref[0, rows, cols] = (o / denom).astype(o_ref.dtype)

    s_prev = scores(units[0])
    for prev, cur in zip(units[:-1], units[1:]):
        s_cur = scores(cur)
        finish(prev, s_prev)
        s_prev = s_cur
    finish(units[-1], s_prev)


def _attention(qkv, batch, seq_len, n_heads, n_pad):
    n_kv = n_heads // GQ
    qkv3 = qkv.reshape(batch, seq_len, (n_heads + 2 * n_kv) * HEAD_DIM)
    k_cols = qkv3[:, :, n_heads * HEAD_DIM:(n_heads + n_kv) * HEAD_DIM]
    kt = k_cols.reshape(batch, seq_len, n_kv, HEAD_DIM).transpose(0, 2, 3, 1)
    tq = _pick(seq_len, (704, 384, 128))
    n_sub = 4
    qw = GQ * HEAD_DIM
    v_off = n_heads + n_kv
    return pl.pallas_call(
        functools.partial(_attn_kernel, n_pad=n_pad, n_sub=n_sub),
        grid=(batch, n_kv, seq_len // tq),
        in_specs=[
            pl.BlockSpec((1, tq, qw), lambda b, h, i: (b, i, h)),
            pl.BlockSpec((1, 1, HEAD_DIM, seq_len), lambda b, h, i: (b, h, 0, 0)),
            pl.BlockSpec((1, seq_len, HEAD_DIM), lambda b, h, i: (b, 0, v_off + h)),
        ],
        out_specs=pl.BlockSpec((1, tq, qw), lambda b, h, i: (b, i, h)),
        out_shape=jax.ShapeDtypeStruct((batch, seq_len, n_heads * HEAD_DIM), BF16),
        compiler_params=_params("parallel", "parallel", "arbitrary"),
        name="attn_core",
    )(qkv3, kt, qkv3)


def _conv_kernel(u_ref, w_ref, b_ref, o_ref, pad_ref, *, n_pad, halo):
    seq = u_ref.shape[1]
    tc = u_ref.shape[2]
    taps = w_ref.shape[0]
    half = (taps - 1) // 2
    nc = seq // CHUNK
    zeros = jnp.zeros((halo, tc), F32)
    pad_ref[pl.ds(0, halo), :] = zeros
    pad_ref[pl.ds(halo + seq, halo), :] = zeros

    def valid_rows(c):
        rows = c * CHUNK + lax.broadcasted_iota(jnp.int32, (CHUNK, 1), 0)
        return (rows >= n_pad).astype(F32)

    def fill(c, carry):
        r0 = pl.multiple_of(c * CHUNK, CHUNK)
        pad_ref[pl.ds(halo + r0, CHUNK), :] = u_ref[0, pl.ds(r0, CHUNK), :] * valid_rows(c)
        return carry

    lax.fori_loop(0, nc, fill, 0)

    def conv(c, carry):
        r0 = pl.multiple_of(c * CHUNK, CHUNK)
        window = pad_ref[pl.ds(r0, CHUNK + 2 * halo), :]
        acc = jnp.broadcast_to(b_ref[...], (CHUNK, tc))
        for t in range(taps):
            off = halo - half + t
            acc = acc + w_ref[pl.ds(t, 1), :] * window[off:off + CHUNK, :]
        o_ref[0, pl.ds(r0, CHUNK), :] = (_silu(acc) * valid_rows(c)).astype(o_ref.dtype)
        return carry

    lax.fori_loop(0, nc, conv, 0)


def _ssd_conv(zx3, conv_w, conv_b, d_inner, n_pad):
    batch, seq_len, _ = zx3.shape
    taps, ch = conv_w.shape
    tc = _pick(ch, (512, 256, 128))
    assert d_inner % tc == 0
    col0 = d_inner // tc
    halo = 8
    return pl.pallas_call(
        functools.partial(_conv_kernel, n_pad=n_pad, halo=halo),
        grid=(batch, ch // tc),
        in_specs=[
            pl.BlockSpec((1, seq_len, tc), lambda b, j: (b, 0, col0 + j)),
            pl.BlockSpec((taps, tc), lambda b, j: (0, j)),
            pl.BlockSpec((1, tc), lambda b, j: (0, j)),
        ],
        out_specs=pl.BlockSpec((1, seq_len, tc), lambda b, j: (b, 0, j)),
        out_shape=jax.ShapeDtypeStruct((batch, seq_len, ch), BF16),
        scratch_shapes=[pltpu.VMEM((seq_len + 2 * halo, tc), F32)],
        compiler_params=_params("parallel", "parallel"),
        name="ssd_conv",
    )(zx3, conv_w, conv_b.reshape(1, ch))


def _softplus(x):
    return jnp.maximum(x, 0.0) + jnp.log(1.0 + jnp.exp(-jnp.abs(x)))


def _ssd_kernel(xs_ref, b_ref, c_ref, dtr_ref, pc_ref, dskip_ref, y_ref, st_ref, xw_ref, tabt_ref, rows_ref,
                *, n_pad, hpg):
    seq = xs_ref.shape[1]
    gw = xs_ref.shape[2]
    nc = seq // CHUNK
    nh = 2 * hpg
    pw = 2 * SSD_HEAD_DIM
    hi = lax.Precision.HIGHEST

    row_i = lax.broadcasted_iota(jnp.int32, (CHUNK, CHUNK), 0)
    col_i = lax.broadcasted_iota(jnp.int32, (CHUNK, CHUNK), 1)
    lower = row_i >= col_i
    upper = row_i <= col_i
    eye = row_i == col_i
    lower_f = lower.astype(F32)
    upper_f = upper.astype(F32)
    first_of_pair = lax.broadcasted_iota(jnp.int32, (1, pw), 1) < SSD_HEAD_DIM
    tab_row = lax.broadcasted_iota(jnp.int32, (CHUNK, gw), 0)
    lane_head = lax.shift_right_logical(lax.broadcasted_iota(jnp.int32, (CHUNK, gw), 1),
                                        SSD_HEAD_DIM.bit_length() - 1)
    expand = [(tab_row == nh + d * hpg + lane_head).astype(BF16) for d in range(2)]

    bias_col = pc_ref[0, :, pl.ds(0, 1)]
    aneg2_col = -jnp.exp(pc_ref[0, :, pl.ds(1, 1)]) * LOG2E
    dskip = dskip_ref[0]

    def tables_head(c, d):
        lanes = c * CHUNK + lax.broadcasted_iota(jnp.int32, (1, CHUNK), 1)
        dt_row = _softplus(dtr_ref[0, 0, c] + bias_col) * (lanes >= n_pad).astype(F32)
        a_row = dt_row * aneg2_col
        log2_dt = jnp.log2(dt_row)
        tri_f = upper_f if d == 0 else lower_f
        cs_row = jnp.dot(a_row, tri_f, precision=hi, preferred_element_type=F32)
        total = jnp.sum(a_row, axis=1, keepdims=True)
        upd_row = (total - cs_row) + log2_dt
        src_row = cs_row - log2_dt
        table_t = jnp.concatenate(
            [cs_row, upd_row, jnp.zeros((CHUNK - 2 * nh, CHUNK), F32)], axis=0).T
        return cs_row, src_row, jnp.broadcast_to(jnp.exp2(total), (nh, CHUNK)), table_t

    def tables_tail(c, d, slot, tables):
        cs_row, src_row, decay_rows, table_t = tables
        r0 = pl.multiple_of(c * CHUNK, CHUNK)
        w_exp = jnp.dot(jnp.exp2(table_t).astype(BF16), expand[d], preferred_element_type=F32)
        x = xs_ref[0, pl.ds(r0, CHUNK), :]
        xw_ref[slot, d] = (x.astype(F32) * w_exp).astype(BF16)
        tabt_ref[slot, d] = table_t
        rows_ref[slot, d, pl.ds(0, nh), :] = cs_row
        rows_ref[slot, d, pl.ds(nh, nh), :] = src_row
        rows_ref[slot, d, pl.ds(2 * nh, nh), :] = decay_rows

    def scan_head(c, d, slot):
        r0 = pl.multiple_of(c * CHUNK, CHUNK)
        x = xs_ref[0, pl.ds(r0, CHUNK), :]
        bm = b_ref[0, pl.ds(r0, CHUNK), :]
        cm = c_ref[0, pl.ds(r0, CHUNK), :]
        cb = lax.dot_general(cm, bm, (((1,), (1,)), ((), ())), preferred_element_type=F32)
        bt = bm.astype(F32).T.astype(BF16)
        upd = jnp.dot(bt, xw_ref[slot, d], preferred_element_type=F32)
        state = st_ref[d]
        z = jnp.dot(cm, state.astype(BF16), preferred_element_type=F32)
        return r0, x, cb, upd, state, z

    def scan_tail(d, slot, head):
        r0, x, cb, upd, state, z = head
        tri = lower if d == 0 else upper
        table_t = tabt_ref[slot, d]
        cs_row = rows_ref[slot, d, pl.ds(0, nh), :]
        src_row = rows_ref[slot, d, pl.ds(nh, nh), :]
        chunk_decay = rows_ref[slot, d, pl.ds(2 * nh, nh), :]
        xz = jnp.concatenate([x, z.astype(BF16)], axis=0)
        ys = []
        new_states = []
        for pair in range(hpg // 2):
            cols = slice(pair * pw, (pair + 1) * pw)
            rhs = xz[:, cols]
            h0 = d * hpg + 2 * pair
            outs = []
            for hh in (h0, h0 + 1):
                seg = jnp.where(tri, table_t[:, hh:hh + 1] - src_row[hh:hh + 1, :], -jnp.inf)
                m_diag = cb * jnp.exp2(seg)
                carry_in = jnp.where(eye, jnp.exp2(cs_row[hh:hh + 1, :]), 0.0)
                lhs = jnp.concatenate([m_diag, carry_in], axis=1).astype(BF16)
                outs.append(jnp.dot(lhs, rhs, preferred_element_type=F32))
            ys.append(jnp.where(first_of_pair, outs[0], outs[1]))
            decay = jnp.where(first_of_pair, chunk_decay[h0:h0 + 1, :], chunk_decay[h0 + 1:h0 + 2, :])
            new_states.append(decay * state[:, cols] + upd[:, cols])
        st_ref[d] = jnp.concatenate(new_states, axis=1)
        y = jnp.concatenate(ys, axis=1)
        if d == 0:
            y = y + dskip * x.astype(F32)
        y_ref[0, pl.ds(r0, CHUNK), :] += y

    def step(i, slot, nxt):
        heads = [scan_head(i, 0, slot), scan_head(nc - 1 - i, 1, slot)]
        if nxt is not None:
            tabs = [tables_head(nxt, 0), tables_head(nc - 1 - nxt, 1)]
        scan_tail(0, slot, heads[0])
        scan_tail(1, slot, heads[1])
        if nxt is not None:
            tables_tail(nxt, 0, 1 - slot, tabs[0])
            tables_tail(nc - 1 - nxt, 1, 1 - slot, tabs[1])

    y_ref[...] = jnp.zeros(y_ref.shape, F32)
    st_ref[...] = jnp.zeros(st_ref.shape, F32)
    tables_tail(0, 0, 0, tables_head(0, 0))
    tables_tail(nc - 1, 1, 0, tables_head(nc - 1, 1))

    def body(k, carry):
        i = 2 * k
        step(i, 0, i + 1)
        step(i + 1, 1, jnp.minimum(i + 2, nc - 1))
        return carry

    lax.fori_loop(0, nc // 2, body, 0)
    if nc % 2:
        step(nc - 1, 0, None)


def _ssd_core(xbc, dt_raw, dt_bias, a_log, d_skip, batch, seq_len, d_inner, n_pad):
    groups = SSD_GROUPS
    heads = d_inner // SSD_HEAD_DIM
    hpg = heads // groups
    gw = hpg * SSD_HEAD_DIM
    nc = seq_len // CHUNK
    assert gw % (2 * SSD_HEAD_DIM) == 0 and gw % LANES == 0
    b_off = d_inner // SSD_STATE
    c_off = b_off + groups
    dt_rowform = dt_raw.reshape(batch, nc, CHUNK, 2, groups, hpg).transpose(0, 4, 1, 3, 5, 2)
    dt_rowform = dt_rowform.reshape(batch, groups, nc, 2 * hpg, CHUNK)
    per_head = jnp.stack([dt_bias.reshape(2, groups, hpg), a_log.reshape(2, groups, hpg)], -1)
    p_col = per_head.transpose(1, 0, 2, 3).reshape(groups, 2 * hpg, 2).astype(F32)
    d_exp = jnp.repeat(d_skip.astype(F32), SSD_HEAD_DIM).reshape(groups, 1, gw)
    return pl.pallas_call(
        functools.partial(_ssd_kernel, n_pad=n_pad, hpg=hpg),
        grid=(batch, groups),
        in_specs=[
            pl.BlockSpec((1, seq_len, gw), lambda b, g: (b, 0, g)),
            pl.BlockSpec((1, seq_len, SSD_STATE), lambda b, g: (b, 0, b_off + g)),
            pl.BlockSpec((1, seq_len, SSD_STATE), lambda b, g: (b, 0, c_off + g)),
            pl.BlockSpec((1, 1, nc, 2 * hpg, CHUNK), lambda b, g: (b, g, 0, 0, 0)),
            pl.BlockSpec((1, 2 * hpg, 2), lambda b, g: (g, 0, 0)),
            pl.BlockSpec((1, 1, gw), lambda b, g: (g, 0, 0)),
        ],
        out_specs=pl.BlockSpec((1, seq_len, gw), lambda b, g: (b, 0, g)),
        out_shape=jax.ShapeDtypeStruct((batch, seq_len, d_inner), F32),
        scratch_shapes=[
            pltpu.VMEM((2, SSD_STATE, gw), F32),
            pltpu.VMEM((2, 2, CHUNK, gw), BF16),
            pltpu.VMEM((2, 2, CHUNK, CHUNK), F32),
            pltpu.VMEM((2, 2, 6 * hpg, CHUNK), F32),
        ],
        compiler_params=_params("parallel", "parallel"),
        name="ssd_core",
    )(xbc, xbc, xbc, dt_rowform, p_col, d_exp)


def _gate_out_kernel(y_ref, z_ref, nw_ref, w_ref, r_ref, o_ref, g_ref, *, group_w):
    @pl.when(pl.program_id(1) == 0)
    def _():
        for gi in range(y_ref.shape[1] // group_w):
            sl = slice(gi * group_w, (gi + 1) * group_w)
            gated = y_ref[:, sl] * _silu(z_ref[:, sl])
            g_ref[:, sl] = _rms_rows(gated, nw_ref[:, sl]).astype(BF16)

    o_ref[...] = r_ref[...] + jnp.dot(g_ref[...], w_ref[...], preferred_element_type=F32)


def _gate_out(y, zx, norm_w, w, res, d_inner):
    m = y.shape[0]
    n = w.shape[1]
    tm = _pick(m, (384, 256, 128))
    tn = _pick(n, (512, 256, 128))
    return pl.pallas_call(
        functools.partial(_gate_out_kernel, group_w=d_inner // SSD_GROUPS),
        grid=(m // tm, n // tn),
        in_specs=[
            pl.BlockSpec((tm, d_inner), lambda i, j: (i, 0)),
            pl.BlockSpec((tm, d_inner), lambda i, j: (i, 0)),
            pl.BlockSpec((1, d_inner), lambda i, j: (0, 0)),
            pl.BlockSpec((d_inner, tn), lambda i, j: (0, j)),
            pl.BlockSpec((tm, tn), lambda i, j: (i, j)),
        ],
        out_specs=pl.BlockSpec((tm, tn), lambda i, j: (i, j)),
        out_shape=jax.ShapeDtypeStruct((m, n), F32),
        scratch_shapes=[pltpu.VMEM((tm, d_inner), BF16)],
        compiler_params=_params("parallel", "arbitrary"),
        name="ssd_gate_out",
    )(y, zx, norm_w.reshape(1, d_inner), w, res)


def _ssd_mixer(h, batch, seq_len, n_pad, mix_norm, in_proj, conv_w, conv_b, dt_bias, a_log, d_skip,
               norm_w, out_proj):
    d_inner = out_proj.shape[0]
    heads = d_inner // SSD_HEAD_DIM
    n_zx = in_proj.shape[1] - 2 * heads
    zx = _norm_matmul(h, mix_norm, in_proj[:, :n_zx].astype(BF16), F32, "ssd_in_zx")
    dt_raw = _norm_matmul(h, mix_norm, in_proj[:, n_zx:].astype(BF16), F32, "ssd_in_dt")
    xbc = _ssd_conv(zx.reshape(batch, seq_len, n_zx), conv_w, conv_b, d_inner, n_pad)
    y = _ssd_core(xbc, dt_raw, dt_bias, a_log, d_skip, batch, seq_len, d_inner, n_pad)
    return _gate_out(y.reshape(batch * seq_len, d_inner), zx, norm_w, out_proj.astype(BF16), h, d_inner)


def _rope_tables(seq_len, n_pad, n_meta):
    n_tok = seq_len - n_pad - n_meta
    rows_n = n_tok // GRID_W
    row = jnp.concatenate([jnp.zeros((n_pad,), jnp.int32), jnp.full((n_meta,), -1, jnp.int32),
                           jnp.repeat(jnp.arange(rows_n, dtype=jnp.int32), GRID_W)])
    col = jnp.concatenate([jnp.zeros((n_pad,), jnp.int32), jnp.arange(n_meta, dtype=jnp.int32),
                           jnp.tile(jnp.arange(GRID_W, dtype=jnp.int32), rows_n)])
    inv_freq = ROPE_THETA ** (-jnp.arange(0, 2 * ROPE_HALF, 2, dtype=F32) / (2 * ROPE_HALF))
    ang_r = row.astype(F32)[:, None] * inv_freq
    ang_c = col.astype(F32)[:, None] * inv_freq
    cos = jnp.concatenate([jnp.cos(ang_r), jnp.cos(ang_r), jnp.cos(ang_c), jnp.cos(ang_c)], -1)
    sin = jnp.concatenate([-jnp.sin(ang_r), jnp.sin(ang_r), -jnp.sin(ang_c), jnp.sin(ang_c)], -1)
    return cos, sin


def _attention_mixer(h, batch, seq_len, n_pad, cos, sin, mix_norm, w_qkv, q_norm, k_norm, w_o):
    n_heads = w_o.shape[0] // HEAD_DIM
    n_kv = n_heads // GQ
    head_w = jnp.concatenate([jnp.tile(q_norm.astype(F32) * (HEAD_DIM ** -0.5 * LOG2E), n_heads),
                              jnp.tile(k_norm.astype(F32), n_kv),
                              jnp.ones((n_kv * HEAD_DIM,), F32)]).reshape(1, -1)
    qkv = _qkv_proj(h, mix_norm, w_qkv.astype(BF16), head_w, cos, sin, seq_len,
                    (n_heads + n_kv) * HEAD_DIM)
    o = _attention(qkv, batch, seq_len, n_heads, n_pad)
    return _matmul_res(o.reshape(batch * seq_len, n_heads * HEAD_DIM), w_o.astype(BF16), h, "attn_out")


def kernel(x, meta_tokens, ffn_norm, ffn_w_in, ffn_w_out, mix_norm, ssd_in_proj, ssd_conv_w, ssd_conv_b,
           ssd_dt_bias, ssd_A_log, ssd_D, ssd_norm, ssd_out_proj, attn_w_qkv, attn_q_norm, attn_k_norm,
           attn_w_o):
    batch, n_tok, d = x.shape
    n_meta = meta_tokens.shape[0]
    n_pad = CHUNK - n_meta
    seq_len = n_pad + n_meta + n_tok
    depth = ffn_norm.shape[0]
    cos, sin = _rope_tables(seq_len, n_pad, n_meta)
    h = jnp.concatenate([jnp.zeros((batch, n_pad, d), x.dtype),
                         jnp.broadcast_to(meta_tokens.astype(x.dtype)[None], (batch, n_meta, d)),
                         x], axis=1).reshape(batch * seq_len, d)
    for i in range(depth):
        j = i // 2
        h = _ffn(h, ffn_norm[i, 0], ffn_w_in[i, 0].astype(BF16), ffn_w_out[i, 0].astype(BF16))
        if i % 2 == 0:
            h = _ssd_mixer(h, batch, seq_len, n_pad, mix_norm[i], ssd_in_proj[j], ssd_conv_w[j],
                           ssd_conv_b[j], ssd_dt_bias[j], ssd_A_log[j], ssd_D[j], ssd_norm[j],
                           ssd_out_proj[j])
        else:
            h = _attention_mixer(h, batch, seq_len, n_pad, cos, sin, mix_norm[i], attn_w_qkv[j],
                                 attn_q_norm[j], attn_k_norm[j], attn_w_o[j])
        h = _ffn(h, ffn_norm[i, 1], ffn_w_in[i, 1].astype(BF16), ffn_w_out[i, 1].astype(BF16))
    return h.reshape(batch, seq_len, d)[:, n_pad + n_meta:, :]
```

```python
import functools

import jax
import jax.numpy as jnp
from jax import lax
from jax.experimental import pallas as pl
from jax.experimental.pallas import tpu as pltpu

F32 = jnp.float32
BF16 = jnp.bfloat16

GRID_W = 64
CHUNK = 128
EPS = 1e-6
SSD_HEAD_DIM = 64
SSD_GROUPS = 8
SSD_STATE = 128
HEAD_DIM = 128
GQ = 2
ROPE_THETA = 10000.0
ROPE_HALF = HEAD_DIM // 4
LOG2E = 1.4426950408889634

V7X_VMEM_LIMIT_BYTES = 56 * 1024 * 1024
LANES = 128
SUBLANES = 8


def _params(*sem):
    return pltpu.CompilerParams(dimension_semantics=sem, vmem_limit_bytes=V7X_VMEM_LIMIT_BYTES)


def _pick(n, candidates):
    for c in candidates:
        if n % c == 0:
            return c
    return n


def _rms_rows(x, w):
    ms = jnp.mean(x * x, axis=-1, keepdims=True)
    return x * lax.rsqrt(ms + EPS) * w


def _silu(x):
    return x / (1.0 + jnp.exp(-x))


def _ffn_kernel(x_ref, nw_ref, wg_ref, wu_ref, wo_ref, o_ref, hn_ref):
    j = pl.program_id(1)

    @pl.when(j == 0)
    def _():
        x = x_ref[...]
        hn_ref[...] = _rms_rows(x, nw_ref[...]).astype(BF16)
        o_ref[...] = x

    hn = hn_ref[...]
    g = jnp.dot(hn, wg_ref[...], preferred_element_type=F32)
    u = jnp.dot(hn, wu_ref[...], preferred_element_type=F32)
    a = (0.5 * _silu(g) * u).astype(BF16)
    o_ref[...] += jnp.dot(a, wo_ref[...], preferred_element_type=F32)


def _ffn(h, norm_w, w_in, w_out):
    m, d = h.shape
    f = w_out.shape[0]
    tm = _pick(m, (768, 384, 256, 128))
    tf = _pick(f, (512, 256, 128))
    nf = f // tf
    return pl.pallas_call(
        _ffn_kernel,
        grid=(m // tm, nf),
        in_specs=[
            pl.BlockSpec((tm, d), lambda i, j: (i, 0)),
            pl.BlockSpec((1, d), lambda i, j: (0, 0)),
            pl.BlockSpec((d, tf), lambda i, j: (0, j)),
            pl.BlockSpec((d, tf), lambda i, j: (0, j + nf)),
            pl.BlockSpec((tf, d), lambda i, j: (j, 0)),
        ],
        out_specs=pl.BlockSpec((tm, d), lambda i, j: (i, 0)),
        out_shape=jax.ShapeDtypeStruct((m, d), F32),
        scratch_shapes=[pltpu.VMEM((tm, d), BF16)],
        compiler_params=_params("parallel", "arbitrary"),
        name="ffn",
    )(h, norm_w.reshape(1, d), w_in, w_in, w_out)


def _norm_matmul_kernel(x_ref, nw_ref, w_ref, o_ref, hn_ref):
    @pl.when(pl.program_id(1) == 0)
    def _():
        hn_ref[...] = _rms_rows(x_ref[...], nw_ref[...]).astype(BF16)

    o_ref[...] = jnp.dot(hn_ref[...], w_ref[...], preferred_element_type=F32).astype(o_ref.dtype)


def _norm_matmul(h, norm_w, w, out_dtype, name):
    m, d = h.shape
    n = w.shape[1]
    tm = _pick(m, (768, 384, 256, 128))
    tn = _pick(n, (1024, 512, 256, 128))
    return pl.pallas_call(
        _norm_matmul_kernel,
        grid=(m // tm, n // tn),
        in_specs=[
            pl.BlockSpec((tm, d), lambda i, j: (i, 0)),
            pl.BlockSpec((1, d), lambda i, j: (0, 0)),
            pl.BlockSpec((d, tn), lambda i, j: (0, j)),
        ],
        out_specs=pl.BlockSpec((tm, tn), lambda i, j: (i, j)),
        out_shape=jax.ShapeDtypeStruct((m, n), out_dtype),
        scratch_shapes=[pltpu.VMEM((tm, d), BF16)],
        compiler_params=_params("parallel", "arbitrary"),
        name=name,
    )(h, norm_w.reshape(1, d), w)


def _matmul_res_kernel(a_ref, w_ref, r_ref, o_ref):
    o_ref[...] = r_ref[...] + jnp.dot(a_ref[...], w_ref[...], preferred_element_type=F32)


def _matmul_res(a, w, res, name):
    m, k = a.shape
    n = w.shape[1]
    tm = _pick(m, (768, 384, 256, 128))
    tn = _pick(n, (1024, 512, 256, 128))
    return pl.pallas_call(
        _matmul_res_kernel,
        grid=(m // tm, n // tn),
        in_specs=[
            pl.BlockSpec((tm, k), lambda i, j: (i, 0)),
            pl.BlockSpec((k, tn), lambda i, j: (0, j)),
            pl.BlockSpec((tm, tn), lambda i, j: (i, j)),
        ],
        out_specs=pl.BlockSpec((tm, tn), lambda i, j: (i, j)),
        out_shape=jax.ShapeDtypeStruct((m, n), F32),
        compiler_params=_params("parallel", "parallel"),
        name=name,
    )(a, w, res)


def _qkv_kernel(x_ref, nw_ref, w_ref, hw_ref, cos_ref, sin_ref, o_ref, hn_ref, *, n_rope_tiles):
    j = pl.program_id(1)

    @pl.when(j == 0)
    def _():
        hn_ref[...] = _rms_rows(x_ref[...], nw_ref[...]).astype(BF16)

    tn = w_ref.shape[1]
    sub_w = 2 * HEAD_DIM

    @pl.when(j < n_rope_tiles)
    def _():
        hn = hn_ref[...]
        c = cos_ref[...]
        s = sin_ref[...]
        lane = lax.broadcasted_iota(jnp.int32, (1, HEAD_DIM), 1)
        first_half = (lane % (2 * ROPE_HALF)) < ROPE_HALF

        def product(p0):
            return jnp.dot(hn, w_ref[:, p0:p0 + sub_w], preferred_element_type=F32)

        def norm_rope(p0, acc):
            for h0 in range(0, sub_w, HEAD_DIM):
                sl = slice(p0 + h0, p0 + h0 + HEAD_DIM)
                yn = _rms_rows(acc[:, h0:h0 + HEAD_DIM], hw_ref[:, sl])
                partner = jnp.where(first_half,
                                    pltpu.roll(yn, HEAD_DIM - ROPE_HALF, 1),
                                    pltpu.roll(yn, ROPE_HALF, 1))
                o_ref[:, sl] = (yn * c + partner * s).astype(o_ref.dtype)

        starts = list(range(0, tn, sub_w))
        acc = product(starts[0])
        for prev, cur in zip(starts[:-1], starts[1:]):
            nxt = product(cur)
            norm_rope(prev, acc)
            acc = nxt
        norm_rope(starts[-1], acc)

    @pl.when(j >= n_rope_tiles)
    def _():
        o_ref[...] = jnp.dot(hn_ref[...], w_ref[...], preferred_element_type=F32).astype(o_ref.dtype)


def _qkv_proj(h, norm_w, w, head_w, cos, sin, seq_len, n_rope_cols):
    m, d = h.shape
    n = w.shape[1]
    tm = _pick(seq_len, (1408, 384, 128))
    tn = _pick(n_rope_cols, (1024, 512, 256))
    assert n % tn == 0 and n_rope_cols % tn == 0
    tiles_per_seq = seq_len // tm
    kern = functools.partial(_qkv_kernel, n_rope_tiles=n_rope_cols // tn)
    return pl.pallas_call(
        kern,
        grid=(m // tm, n // tn),
        in_specs=[
            pl.BlockSpec((tm, d), lambda i, j: (i, 0)),
            pl.BlockSpec((1, d), lambda i, j: (0, 0)),
            pl.BlockSpec((d, tn), lambda i, j: (0, j)),
            pl.BlockSpec((1, tn), lambda i, j: (0, j)),
            pl.BlockSpec((tm, HEAD_DIM), lambda i, j: (i % tiles_per_seq, 0)),
            pl.BlockSpec((tm, HEAD_DIM), lambda i, j: (i % tiles_per_seq, 0)),
        ],
        out_specs=pl.BlockSpec((tm, tn), lambda i, j: (i, j)),
        out_shape=jax.ShapeDtypeStruct((m, n), BF16),
        scratch_shapes=[pltpu.VMEM((tm, d), BF16)],
        compiler_params=_params("parallel", "arbitrary"),
        name="attn_qkv",
    )(h, norm_w.reshape(1, d), w, head_w, cos, sin)


def _attn_kernel(q_ref, kt_ref, v_ref, o_ref, *, n_pad, n_sub):
    kt = kt_ref[0, 0]
    v = v_ref[0]
    rs = q_ref.shape[1] // n_sub
    key_valid = lax.broadcasted_iota(jnp.int32, (1, LANES), 1) >= n_pad
    assert n_pad <= LANES
    units = [(slice(r * rs, (r + 1) * rs), slice(g * HEAD_DIM, (g + 1) * HEAD_DIM))
             for g in range(GQ) for r in range(n_sub)]

    def scores(unit):
        rows, cols = unit
        return jnp.dot(q_ref[0, rows, cols], kt, preferred_element_type=F32)

    def finish(unit, s):
        rows, cols = unit
        s = jnp.concatenate([jnp.where(key_valid, s[:, :LANES], -jnp.inf), s[:, LANES:]], axis=1)
        p = jnp.exp2(s - jnp.max(s, axis=-1, keepdims=True))
        denom = jnp.sum(p, axis=-1, keepdims=True)
        o = jnp.dot(p.astype(BF16), v, preferred_element_type=F32)
        o_ref[0, rows, cols] = (o / denom).astype(o_ref.dtype)

    s_prev = scores(units[0])
    for prev, cur in zip(units[:-1], units[1:]):
        s_cur = scores(cur)
        finish(prev, s_prev)
        s_prev = s_cur
    finish(units[-1], s_prev)


def _attention(qkv, batch, seq_len, n_heads, n_pad):
    n_kv = n_heads // GQ
    qkv3 = qkv.reshape(batch, seq_len, (n_heads + 2 * n_kv) * HEAD_DIM)
    k_cols = qkv3[:, :, n_heads * HEAD_DIM:(n_heads + n_kv) * HEAD_DIM]
    kt = k_cols.reshape(batch, seq_len, n_kv, HEAD_DIM).transpose(0, 2, 3, 1)
    tq = _pick(seq_len, (704, 384, 128))
    n_sub = 4
    qw = GQ * HEAD_DIM
    v_off = n_heads + n_kv
    return pl.pallas_call(
        functools.partial(_attn_kernel, n_pad=n_pad, n_sub=n_sub),
        grid=(batch, n_kv, seq_len // tq),
        in_specs=[
            pl.BlockSpec((1, tq, qw), lambda b, h, i: (b, i, h)),
            pl.BlockSpec((1, 1, HEAD_DIM, seq_len), lambda b, h, i: (b, h, 0, 0)),
            pl.BlockSpec((1, seq_len, HEAD_DIM), lambda b, h, i: (b, 0, v_off + h)),
        ],
        out_specs=pl.BlockSpec((1, tq, qw), lambda b, h, i: (b, i, h)),
        out_shape=jax.ShapeDtypeStruct((batch, seq_len, n_heads * HEAD_DIM), BF16),
        compiler_params=_params("parallel", "parallel", "arbitrary"),
        name="attn_core",
    )(qkv3, kt, qkv3)


def _conv_kernel(u_ref, w_ref, b_ref, o_ref, pad_ref, wb_ref, *, n_pad, halo):
    seq = u_ref.shape[1]
    tc = u_ref.shape[2]
    taps = w_ref.shape[0]
    half = (taps - 1) // 2
    nc = seq // CHUNK
    zeros = jnp.zeros((halo, tc), F32)
    pad_ref[pl.ds(0, halo), :] = zeros
    pad_ref[pl.ds(halo + seq, halo), :] = zeros

    def valid_rows(c):
        rows = c * CHUNK + lax.broadcasted_iota(jnp.int32, (CHUNK, 1), 0)
        return (rows >= n_pad).astype(F32)

    def fill(c, carry):
        r0 = pl.multiple_of(c * CHUNK, CHUNK)
        pad_ref[pl.ds(halo + r0, CHUNK), :] = u_ref[0, pl.ds(r0, CHUNK), :].astype(F32) * valid_rows(c)
        return carry

    lax.fori_loop(0, nc, fill, 0)

    sub = lax.broadcasted_iota(jnp.int32, (SUBLANES, 1), 0)
    groups = CHUNK // SUBLANES
    for t in range(taps):
        wb_ref[t] = jnp.broadcast_to(w_ref[pl.ds(t, 1), :], (SUBLANES, tc))
    wb_ref[taps] = jnp.broadcast_to(b_ref[...], (SUBLANES, tc))

    def conv(c, carry):
        r0 = pl.multiple_of(c * CHUNK, CHUNK)
        tiles = [pad_ref[pl.ds(r0 + SUBLANES * i, SUBLANES), :] for i in range(groups + 2)]
        fwd = {k: [pltpu.roll(t, SUBLANES - k, 0) for t in tiles] for k in range(1, half + 1)}
        bwd = {k: [pltpu.roll(t, k, 0) for t in tiles] for k in range(1, half + 1)}
        outs = []
        for i in range(groups):
            acc = wb_ref[taps] + wb_ref[half] * tiles[i + 1]
            for k in range(1, half + 1):
                ahead = jnp.where(sub < SUBLANES - k, fwd[k][i + 1], fwd[k][i + 2])
                behind = jnp.where(sub >= k, bwd[k][i + 1], bwd[k][i])
                acc = acc + wb_ref[half + k] * ahead + wb_ref[half - k] * behind
            outs.append(acc)
        acc = jnp.concatenate(outs, axis=0)
        o_ref[0, pl.ds(r0, CHUNK), :] = (_silu(acc) * valid_rows(c)).astype(o_ref.dtype)
        return carry

    lax.fori_loop(0, nc, conv, 0)


def _ssd_conv(zx3, conv_w, conv_b, d_inner, n_pad):
    batch, seq_len, _ = zx3.shape
    taps, ch = conv_w.shape
    tc = _pick(ch, (512, 256, 128))
    assert d_inner % tc == 0
    col0 = d_inner // tc
    halo = SUBLANES
    assert (taps - 1) // 2 < SUBLANES
    return pl.pallas_call(
        functools.partial(_conv_kernel, n_pad=n_pad, halo=halo),
        grid=(batch, ch // tc),
        in_specs=[
            pl.BlockSpec((1, seq_len, tc), lambda b, j: (b, 0, col0 + j)),
            pl.BlockSpec((taps, tc), lambda b, j: (0, j)),
            pl.BlockSpec((1, tc), lambda b, j: (0, j)),
        ],
        out_specs=pl.BlockSpec((1, seq_len, tc), lambda b, j: (b, 0, j)),
        out_shape=jax.ShapeDtypeStruct((batch, seq_len, ch), BF16),
        scratch_shapes=[pltpu.VMEM((seq_len + 2 * halo, tc), F32),
                        pltpu.VMEM((taps + 1, SUBLANES, tc), F32)],
        compiler_params=_params("parallel", "parallel"),
        name="ssd_conv",
    )(zx3, conv_w, conv_b.reshape(1, ch))


def _softplus(x):
    return jnp.maximum(x, 0.0) + jnp.log(1.0 + jnp.exp(-jnp.abs(x)))


def _ssd_kernel(xs_ref, b_ref, c_ref, z_ref, dtr_ref, pc_ref, dskip_ref, nw_ref, g_ref, y_ref, st_ref, xw_ref,
                tabt_ref, rows_ref, *, n_pad, hpg):
    seq = xs_ref.shape[1]
    gw = xs_ref.shape[2]
    nc = seq // CHUNK
    nh = 2 * hpg
    pw = 2 * SSD_HEAD_DIM
    hi = lax.Precision.HIGHEST

    row_i = lax.broadcasted_iota(jnp.int32, (CHUNK, CHUNK), 0)
    col_i = lax.broadcasted_iota(jnp.int32, (CHUNK, CHUNK), 1)
    lower = row_i >= col_i
    upper = row_i <= col_i
    eye = row_i == col_i
    lower_f = lower.astype(F32)
    upper_f = upper.astype(F32)
    first_of_pair = lax.broadcasted_iota(jnp.int32, (1, pw), 1) < SSD_HEAD_DIM
    tab_row = lax.broadcasted_iota(jnp.int32, (CHUNK, gw), 0)
    lane_head = lax.shift_right_logical(lax.broadcasted_iota(jnp.int32, (CHUNK, gw), 1),
                                        SSD_HEAD_DIM.bit_length() - 1)
    expand = [(tab_row == nh + d * hpg + lane_head).astype(BF16) for d in range(2)]

    bias_col = pc_ref[0, :, pl.ds(0, 1)]
    aneg2_col = -jnp.exp(pc_ref[0, :, pl.ds(1, 1)]) * LOG2E
    dskip = dskip_ref[0]

    def tables_head(c, d):
        lanes = c * CHUNK + lax.broadcasted_iota(jnp.int32, (1, CHUNK), 1)
        dt_row = _softplus(dtr_ref[0, 0, c] + bias_col) * (lanes >= n_pad).astype(F32)
        a_row = dt_row * aneg2_col
        log2_dt = jnp.log2(dt_row)
        tri_f = upper_f if d == 0 else lower_f
        cs_row = jnp.dot(a_row, tri_f, precision=hi, preferred_element_type=F32)
        total = jnp.sum(a_row, axis=1, keepdims=True)
        upd_row = (total - cs_row) + log2_dt
        src_row = cs_row - log2_dt
        table_t = jnp.concatenate(
            [cs_row, upd_row, jnp.zeros((CHUNK - 2 * nh, CHUNK), F32)], axis=0).T
        return cs_row, src_row, jnp.broadcast_to(jnp.exp2(total), (nh, CHUNK)), table_t

    def tables_tail(c, d, slot, tables):
        cs_row, src_row, decay_rows, table_t = tables
        r0 = pl.multiple_of(c * CHUNK, CHUNK)
        w_exp = jnp.dot(jnp.exp2(table_t).astype(BF16), expand[d], preferred_element_type=F32)
        x = xs_ref[0, pl.ds(r0, CHUNK), :]
        xw_ref[slot, d] = (x.astype(F32) * w_exp).astype(BF16)
        tabt_ref[slot, d] = table_t
        rows_ref[slot, d, pl.ds(0, nh), :] = cs_row
        rows_ref[slot, d, pl.ds(nh, nh), :] = src_row
        rows_ref[slot, d, pl.ds(2 * nh, nh), :] = decay_rows

    def scan_head(c, d, slot):
        r0 = pl.multiple_of(c * CHUNK, CHUNK)
        x = xs_ref[0, pl.ds(r0, CHUNK), :]
        bm = b_ref[0, pl.ds(r0, CHUNK), :]
        cm = c_ref[0, pl.ds(r0, CHUNK), :]
        cb = lax.dot_general(cm, bm, (((1,), (1,)), ((), ())), preferred_element_type=F32)
        bt = bm.astype(F32).T.astype(BF16)
        upd = jnp.dot(bt, xw_ref[slot, d], preferred_element_type=F32)
        state = st_ref[d]
        z = jnp.dot(cm, state.astype(BF16), preferred_element_type=F32)
        return r0, x, cb, upd, state, z

    def scan_tail(d, slot, head):
        r0, x, cb, upd, state, z = head
        tri = lower if d == 0 else upper
        table_t = tabt_ref[slot, d]
        cs_row = rows_ref[slot, d, pl.ds(0, nh), :]
        src_row = rows_ref[slot, d, pl.ds(nh, nh), :]
        chunk_decay = rows_ref[slot, d, pl.ds(2 * nh, nh), :]
        xz = jnp.concatenate([x, z.astype(BF16)], axis=0)
        ys = []
        new_states = []
        for pair in range(hpg // 2):
            cols = slice(pair * pw, (pair + 1) * pw)
            rhs = xz[:, cols]
            h0 = d * hpg + 2 * pair
            outs = []
            for hh in (h0, h0 + 1):
                seg = jnp.where(tri, table_t[:, hh:hh + 1] - src_row[hh:hh + 1, :], -jnp.inf)
                m_diag = cb * jnp.exp2(seg)
                carry_in = jnp.where(eye, jnp.exp2(cs_row[hh:hh + 1, :]), 0.0)
                lhs = jnp.concatenate([m_diag, carry_in], axis=1).astype(BF16)
                outs.append(jnp.dot(lhs, rhs, preferred_element_type=F32))
            ys.append(jnp.where(first_of_pair, outs[0], outs[1]))
            decay = jnp.where(first_of_pair, chunk_decay[h0:h0 + 1, :], chunk_decay[h0 + 1:h0 + 2, :])
            new_states.append(decay * state[:, cols] + upd[:, cols])
        st_ref[d] = jnp.concatenate(new_states, axis=1)
        y = jnp.concatenate(ys, axis=1)
        if d == 0:
            y = y + dskip * x.astype(F32)
        y_ref[pl.ds(r0, CHUNK), :] += y

    def gate_chunk(c):
        r0 = pl.multiple_of(c * CHUNK, CHUNK)
        gated = y_ref[pl.ds(r0, CHUNK), :] * _silu(z_ref[0, pl.ds(r0, CHUNK), :].astype(F32))
        g_ref[0, pl.ds(r0, CHUNK), :] = _rms_rows(gated, nw_ref[0]).astype(g_ref.dtype)

    def step(i, slot, nxt, gate):
        heads = [scan_head(i, 0, slot), scan_head(nc - 1 - i, 1, slot)]
        if nxt is not None:
            tabs = [tables_head(nxt, 0), tables_head(nc - 1 - nxt, 1)]
        scan_tail(0, slot, heads[0])
        scan_tail(1, slot, heads[1])
        if gate:
            gate_chunk(i)
            gate_chunk(nc - 1 - i)
        if nxt is not None:
            tables_tail(nxt, 0, 1 - slot, tabs[0])
            tables_tail(nc - 1 - nxt, 1, 1 - slot, tabs[1])

    y_ref[...] = jnp.zeros(y_ref.shape, F32)
    st_ref[...] = jnp.zeros(st_ref.shape, F32)
    tables_tail(0, 0, 0, tables_head(0, 0))
    tables_tail(nc - 1, 1, 0, tables_head(nc - 1, 1))

    def two_steps(gate_first, gate_second):
        def body(k, carry):
            i = 2 * k
            step(i, 0, i + 1, gate_first)
            step(i + 1, 1, jnp.minimum(i + 2, nc - 1), gate_second)
            return carry
        return body

    first_gated = nc // 2
    plain_iters = first_gated // 2
    lax.fori_loop(0, plain_iters, two_steps(False, False), 0)
    gated_from = plain_iters
    if first_gated % 2:
        two_steps(False, True)(plain_iters, 0)
        gated_from += 1
    lax.fori_loop(gated_from, nc // 2, two_steps(True, True), 0)
    if nc % 2:
        step(nc - 1, 0, None, True)


def _ssd_core(xbc, zx3, dt_raw, dt_bias, a_log, d_skip, norm_w, batch, seq_len, d_inner, n_pad):
    groups = SSD_GROUPS
    heads = d_inner // SSD_HEAD_DIM
    hpg = heads // groups
    gw = hpg * SSD_HEAD_DIM
    nc = seq_len // CHUNK
    assert gw % (2 * SSD_HEAD_DIM) == 0 and gw % LANES == 0
    assert gw * groups == d_inner
    b_off = d_inner // SSD_STATE
    c_off = b_off + groups
    dt_rowform = dt_raw.reshape(batch, nc, CHUNK, 2, groups, hpg).transpose(0, 4, 1, 3, 5, 2)
    dt_rowform = dt_rowform.reshape(batch, groups, nc, 2 * hpg, CHUNK)
    per_head = jnp.stack([dt_bias.reshape(2, groups, hpg), a_log.reshape(2, groups, hpg)], -1)
    p_col = per_head.transpose(1, 0, 2, 3).reshape(groups, 2 * hpg, 2).astype(F32)
    d_exp = jnp.repeat(d_skip.astype(F32), SSD_HEAD_DIM).reshape(groups, 1, gw)
    return pl.pallas_call(
        functools.partial(_ssd_kernel, n_pad=n_pad, hpg=hpg),
        grid=(batch, groups),
        in_specs=[
            pl.BlockSpec((1, seq_len, gw), lambda b, g: (b, 0, g)),
            pl.BlockSpec((1, seq_len, SSD_STATE), lambda b, g: (b, 0, b_off + g)),
            pl.BlockSpec((1, seq_len, SSD_STATE), lambda b, g: (b, 0, c_off + g)),
            pl.BlockSpec((1, seq_len, gw), lambda b, g: (b, 0, g)),
            pl.BlockSpec((1, 1, nc, 2 * hpg, CHUNK), lambda b, g: (b, g, 0, 0, 0)),
            pl.BlockSpec((1, 2 * hpg, 2), lambda b, g: (g, 0, 0)),
            pl.BlockSpec((1, 1, gw), lambda b, g: (g, 0, 0)),
            pl.BlockSpec((1, 1, gw), lambda b, g: (g, 0, 0)),
        ],
        out_specs=pl.BlockSpec((1, seq_len, gw), lambda b, g: (b, 0, g)),
        out_shape=jax.ShapeDtypeStruct((batch, seq_len, d_inner), BF16),
        scratch_shapes=[
            pltpu.VMEM((seq_len, gw), F32),
            pltpu.VMEM((2, SSD_STATE, gw), F32),
            pltpu.VMEM((2, 2, CHUNK, gw), BF16),
            pltpu.VMEM((2, 2, CHUNK, CHUNK), F32),
            pltpu.VMEM((2, 2, 6 * hpg, CHUNK), F32),
        ],
        compiler_params=_params("parallel", "parallel"),
        name="ssd_core",
    )(xbc, xbc, xbc, zx3, dt_rowform, p_col, d_exp, norm_w.astype(F32).reshape(groups, 1, gw))


def _ssd_mixer(h, batch, seq_len, n_pad, mix_norm, in_proj, conv_w, conv_b, dt_bias, a_log, d_skip,
               norm_w, out_proj):
    d_inner = out_proj.shape[0]
    heads = d_inner // SSD_HEAD_DIM
    n_zx = in_proj.shape[1] - 2 * heads
    zx = _norm_matmul(h, mix_norm, in_proj[:, :n_zx].astype(BF16), BF16, "ssd_in_zx")
    dt_raw = _norm_matmul(h, mix_norm, in_proj[:, n_zx:].astype(BF16), F32, "ssd_in_dt")
    zx3 = zx.reshape(batch, seq_len, n_zx)
    xbc = _ssd_conv(zx3, conv_w, conv_b, d_inner, n_pad)
    g = _ssd_core(xbc, zx3, dt_raw, dt_bias, a_log, d_skip, norm_w, batch, seq_len, d_inner, n_pad)
    return _matmul_res(g.reshape(batch * seq_len, d_inner), out_proj.astype(BF16), h, "ssd_out")


def _rope_tables(seq_len, n_pad, n_meta):
    n_tok = seq_len - n_pad - n_meta
    rows_n = n_tok // GRID_W
    row = jnp.concatenate([jnp.zeros((n_pad,), jnp.int32), jnp.full((n_meta,), -1, jnp.int32),
                           jnp.repeat(jnp.arange(rows_n, dtype=jnp.int32), GRID_W)])
    col = jnp.concatenate([jnp.zeros((n_pad,), jnp.int32), jnp.arange(n_meta, dtype=jnp.int32),
                           jnp.tile(jnp.arange(GRID_W, dtype=jnp.int32), rows_n)])
    inv_freq = ROPE_THETA ** (-jnp.arange(0, 2 * ROPE_HALF, 2, dtype=F32) / (2 * ROPE_HALF))
    ang_r = row.astype(F32)[:, None] * inv_freq
    ang_c = col.astype(F32)[:, None] * inv_freq
    cos = jnp.concatenate([jnp.cos(ang_r), jnp.cos(ang_r), jnp.cos(ang_c), jnp.cos(ang_c)], -1)
    sin = jnp.concatenate([-jnp.sin(ang_r), jnp.sin(ang_r), -jnp.sin(ang_c), jnp.sin(ang_c)], -1)
    return cos, sin


def _attention_mixer(h, batch, seq_len, n_pad, cos, sin, mix_norm, w_qkv, q_norm, k_norm, w_o):
    n_heads = w_o.shape[0] // HEAD_DIM
    n_kv = n_heads // GQ
    head_w = jnp.concatenate([jnp.tile(q_norm.astype(F32) * (HEAD_DIM ** -0.5 * LOG2E), n_heads),
                              jnp.tile(k_norm.astype(F32), n_kv),
                              jnp.ones((n_kv * HEAD_DIM,), F32)]).reshape(1, -1)
    qkv = _qkv_proj(h, mix_norm, w_qkv.astype(BF16), head_w, cos, sin, seq_len,
                    (n_heads + n_kv) * HEAD_DIM)
    o = _attention(qkv, batch, seq_len, n_heads, n_pad)
    return _matmul_res(o.reshape(batch * seq_len, n_heads * HEAD_DIM), w_o.astype(BF16), h, "attn_out")


def kernel(x, meta_tokens, ffn_norm, ffn_w_in, ffn_w_out, mix_norm, ssd_in_proj, ssd_conv_w, ssd_conv_b,
           ssd_dt_bias, ssd_A_log, ssd_D, ssd_norm, ssd_out_proj, attn_w_qkv, attn_q_norm, attn_k_norm,
           attn_w_o):
    batch, n_tok, d = x.shape
    n_meta = meta_tokens.shape[0]
    n_pad = CHUNK - n_meta
    seq_len = n_pad + n_meta + n_tok
    depth = ffn_norm.shape[0]
    cos, sin = _rope_tables(seq_len, n_pad, n_meta)
    h = jnp.concatenate([jnp.zeros((batch, n_pad, d), x.dtype),
                         jnp.broadcast_to(meta_tokens.astype(x.dtype)[None], (batch, n_meta, d)),
                         x], axis=1).reshape(batch * seq_len, d)
    for i in range(depth):
        j = i // 2
        h = _ffn(h, ffn_norm[i, 0], ffn_w_in[i, 0].astype(BF16), ffn_w_out[i, 0].astype(BF16))
        if i % 2 == 0:
            h = _ssd_mixer(h, batch, seq_len, n_pad, mix_norm[i], ssd_in_proj[j], ssd_conv_w[j],
                           ssd_conv_b[j], ssd_dt_bias[j], ssd_A_log[j], ssd_D[j], ssd_norm[j],
                           ssd_out_proj[j])
        else:
            h = _attention_mixer(h, batch, seq_len, n_pad, cos, sin, mix_norm[i], attn_w_qkv[j],
                                 attn_q_norm[j], attn_k_norm[j], attn_w_o[j])
        h = _ffn(h, ffn_norm[i, 1], ffn_w_in[i, 1].astype(BF16), ffn_w_out[i, 1].astype(BF16))
    return h.reshape(batch, seq_len, d)[:, n_pad + n_meta:, :]
```

```python
import functools

import jax
import jax.numpy as jnp
from jax import lax
from jax.experimental import pallas as pl
from jax.experimental.pallas import tpu as pltpu

F32 = jnp.float32
BF16 = jnp.bfloat16

GRID_W = 64
CHUNK = 128
EPS = 1e-6
SSD_HEAD_DIM = 64
SSD_GROUPS = 8
SSD_STATE = 128
HEAD_DIM = 128
GQ = 2
ROPE_THETA = 10000.0
ROPE_HALF = HEAD_DIM // 4
LOG2E = 1.4426950408889634

V7X_VMEM_LIMIT_BYTES = 56 * 1024 * 1024
LANES = 128
SUBLANES = 8
BF16_TILE_ROWS = 16


def _params(*sem):
    return pltpu.CompilerParams(dimension_semantics=sem, vmem_limit_bytes=V7X_VMEM_LIMIT_BYTES)


def _pick(n, candidates):
    for c in candidates:
        if n % c == 0:
            return c
    return n


def _rms_rows(x, w):
    ms = jnp.mean(x * x, axis=-1, keepdims=True)
    return x * lax.rsqrt(ms + EPS) * w


def _silu(x):
    return x / (1.0 + jnp.exp(-x))


def _ffn_kernel(x_ref, nw_ref, wg_ref, wu_ref, wo_ref, *refs, n_casts):
    cast_in, o_ref, cast_out, hn_ref = refs[:n_casts], refs[n_casts], refs[n_casts + 1:-1], refs[-1]
    j = pl.program_id(1)

    @pl.when(j == 0)
    def _():
        x = x_ref[...]
        hn_ref[...] = _rms_rows(x, nw_ref[...]).astype(BF16)
        o_ref[...] = x

    hn = hn_ref[...]
    g = jnp.dot(hn, wg_ref[...], preferred_element_type=F32)
    u = jnp.dot(hn, wu_ref[...], preferred_element_type=F32)
    a = (0.5 * _silu(g) * u).astype(BF16)
    o_ref[...] += jnp.dot(a, wo_ref[...], preferred_element_type=F32)
    for src, dst in zip(cast_in, cast_out):
        dst[...] = src[...].astype(BF16)


class _Cast:
    def __init__(self, array, lead, col0=0, cols=None):
        self.array, self.lead, self.col0 = array, tuple(lead), col0
        self.rows = array.shape[-2]
        self.cols = array.shape[-1] - col0 if cols is None else cols
        assert col0 % self.cols == 0

    def specs(self, n_steps, step_of):
        nb = 1
        while 2 * nb <= n_steps and self.rows % (2 * nb) == 0 and (self.rows // (2 * nb)) % BF16_TILE_ROWS == 0:
            nb *= 2
        rb = self.rows // nb
        lead, cb = self.lead, self.col0 // self.cols

        def block(*grid_idx):
            return jnp.minimum(step_of(*grid_idx), nb - 1)

        in_spec = pl.BlockSpec((None,) * len(lead) + (rb, self.cols), lambda *g: lead + (block(*g), cb))
        out_spec = pl.BlockSpec((rb, self.cols), lambda *g: (block(*g), 0))
        return in_spec, out_spec, jax.ShapeDtypeStruct((self.rows, self.cols), BF16)


def _ffn(h, norm_w, w_in, w_out, casts=()):
    m, d = h.shape
    f = w_out.shape[0]
    tm = _pick(m, (768, 384, 256, 128))
    tf = _pick(f, (512, 256, 128))
    nf = f // tf
    grid = (m // tm, nf)
    cast_specs = [c.specs(grid[0] * grid[1], lambda i, j: i * nf + j) for c in casts]
    outs = pl.pallas_call(
        functools.partial(_ffn_kernel, n_casts=len(casts)),
        grid=grid,
        in_specs=[
            pl.BlockSpec((tm, d), lambda i, j: (i, 0)),
            pl.BlockSpec((1, d), lambda i, j: (0, 0)),
            pl.BlockSpec((d, tf), lambda i, j: (0, j)),
            pl.BlockSpec((d, tf), lambda i, j: (0, j + nf)),
            pl.BlockSpec((tf, d), lambda i, j: (j, 0)),
        ] + [cs[0] for cs in cast_specs],
        out_specs=[pl.BlockSpec((tm, d), lambda i, j: (i, 0))] + [cs[1] for cs in cast_specs],
        out_shape=[jax.ShapeDtypeStruct((m, d), F32)] + [cs[2] for cs in cast_specs],
        scratch_shapes=[pltpu.VMEM((tm, d), BF16)],
        compiler_params=_params("arbitrary", "arbitrary"),
        name="ffn",
    )(h, norm_w.reshape(1, d), w_in, w_in, w_out, *[c.array for c in casts])
    return outs[0], list(outs[1:])


def _norm_matmul_kernel(x_ref, nw_ref, w_ref, o_ref, hn_ref):
    @pl.when(pl.program_id(1) == 0)
    def _():
        hn_ref[...] = _rms_rows(x_ref[...], nw_ref[...]).astype(BF16)

    o_ref[...] = jnp.dot(hn_ref[...], w_ref[...], preferred_element_type=F32).astype(o_ref.dtype)


def _norm_matmul(h, norm_w, w, out_dtype, name):
    m, d = h.shape
    n = w.shape[1]
    tm = _pick(m, (768, 384, 256, 128))
    tn = _pick(n, (1024, 512, 256, 128))
    return pl.pallas_call(
        _norm_matmul_kernel,
        grid=(m // tm, n // tn),
        in_specs=[
            pl.BlockSpec((tm, d), lambda i, j: (i, 0)),
            pl.BlockSpec((1, d), lambda i, j: (0, 0)),
            pl.BlockSpec((d, tn), lambda i, j: (0, j)),
        ],
        out_specs=pl.BlockSpec((tm, tn), lambda i, j: (i, j)),
        out_shape=jax.ShapeDtypeStruct((m, n), out_dtype),
        scratch_shapes=[pltpu.VMEM((tm, d), BF16)],
        compiler_params=_params("parallel", "arbitrary"),
        name=name,
    )(h, norm_w.reshape(1, d), w)


def _matmul_res_kernel(a_ref, w_ref, r_ref, o_ref):
    o_ref[...] = r_ref[...] + jnp.dot(a_ref[...], w_ref[...], preferred_element_type=F32)


def _matmul_res(a, w, res, name):
    m, k = a.shape
    n = w.shape[1]
    tm = _pick(m, (768, 384, 256, 128))
    tn = _pick(n, (1024, 512, 256, 128))
    return pl.pallas_call(
        _matmul_res_kernel,
        grid=(m // tm, n // tn),
        in_specs=[
            pl.BlockSpec((tm, k), lambda i, j: (i, 0)),
            pl.BlockSpec((k, tn), lambda i, j: (0, j)),
            pl.BlockSpec((tm, tn), lambda i, j: (i, j)),
        ],
        out_specs=pl.BlockSpec((tm, tn), lambda i, j: (i, j)),
        out_shape=jax.ShapeDtypeStruct((m, n), F32),
        compiler_params=_params("parallel", "parallel"),
        name=name,
    )(a, w, res)


def _qkv_kernel(x_ref, nw_ref, w_ref, hw_ref, cos_ref, sin_ref, o_ref, hn_ref, *, n_rope_tiles):
    j = pl.program_id(1)

    @pl.when(j == 0)
    def _():
        hn_ref[...] = _rms_rows(x_ref[...], nw_ref[...]).astype(BF16)

    tn = w_ref.shape[1]
    pw = 2 * HEAD_DIM

    @pl.when(j < n_rope_tiles)
    def _():
        acc = jnp.dot(hn_ref[...], w_ref[...], preferred_element_type=F32)
        c = jnp.concatenate([cos_ref[...]] * 2, axis=1)
        s = jnp.concatenate([sin_ref[...]] * 2, axis=1)
        src = lax.broadcasted_iota(jnp.int32, (pw, pw), 0)
        dst = lax.broadcasted_iota(jnp.int32, (pw, pw), 1)
        same_head = (src // HEAD_DIM == dst // HEAD_DIM).astype(BF16)
        partner_of = jnp.where(dst % (2 * ROPE_HALF) < ROPE_HALF, dst + ROPE_HALF, dst - ROPE_HALF)
        swap = (src == partner_of).astype(BF16)
        for p0 in range(0, tn, pw):
            sl = slice(p0, p0 + pw)
            y = acc[:, sl]
            sumsq = jnp.dot((y * y).astype(BF16), same_head, preferred_element_type=F32)
            a = y * hw_ref[:, sl]
            partner = jnp.dot(a.astype(BF16), swap, preferred_element_type=F32)
            scale = lax.rsqrt(sumsq * (1.0 / HEAD_DIM) + EPS)
            o_ref[:, sl] = ((a * c + partner * s) * scale).astype(o_ref.dtype)

    @pl.when(j >= n_rope_tiles)
    def _():
        o_ref[...] = jnp.dot(hn_ref[...], w_ref[...], preferred_element_type=F32).astype(o_ref.dtype)


def _qkv_proj(h, norm_w, w, head_w, cos, sin, seq_len, n_rope_cols):
    m, d = h.shape
    n = w.shape[1]
    tm = _pick(seq_len, (1408, 384, 128))
    tn = _pick(n_rope_cols, (512, 256))
    assert n % tn == 0 and n_rope_cols % tn == 0
    tiles_per_seq = seq_len // tm
    kern = functools.partial(_qkv_kernel, n_rope_tiles=n_rope_cols // tn)
    return pl.pallas_call(
        kern,
        grid=(m // tm, n // tn),
        in_specs=[
            pl.BlockSpec((tm, d), lambda i, j: (i, 0)),
            pl.BlockSpec((1, d), lambda i, j: (0, 0)),
            pl.BlockSpec((d, tn), lambda i, j: (0, j)),
            pl.BlockSpec((1, tn), lambda i, j: (0, j)),
            pl.BlockSpec((tm, HEAD_DIM), lambda i, j: (i % tiles_per_seq, 0)),
            pl.BlockSpec((tm, HEAD_DIM), lambda i, j: (i % tiles_per_seq, 0)),
        ],
        out_specs=pl.BlockSpec((tm, tn), lambda i, j: (i, j)),
        out_shape=jax.ShapeDtypeStruct((m, n), BF16),
        scratch_shapes=[pltpu.VMEM((tm, d), BF16)],
        compiler_params=_params("parallel", "arbitrary"),
        name="attn_qkv",
    )(h, norm_w.reshape(1, d), w, head_w, cos, sin)


def _attn_kernel(q_ref, kt_ref, v_ref, o_ref, *, n_pad, n_sub):
    kt = kt_ref[0, 0]
    v = v_ref[0]
    rs = q_ref.shape[1] // n_sub
    key_valid = lax.broadcasted_iota(jnp.int32, (1, LANES), 1) >= n_pad
    assert n_pad <= LANES
    units = [(slice(r * rs, (r + 1) * rs), slice(g * HEAD_DIM, (g + 1) * HEAD_DIM))
             for g in range(GQ) for r in range(n_sub)]

    def scores(unit):
        rows, cols = unit
        return jnp.dot(q_ref[0, rows, cols], kt, preferred_element_type=F32)

    def finish(unit, s):
        rows, cols = unit
        s = jnp.concatenate([jnp.where(key_valid, s[:, :LANES], -jnp.inf), s[:, LANES:]], axis=1)
        p = jnp.exp2(s - jnp.max(s, axis=-1, keepdims=True))
        denom = jnp.sum(p, axis=-1, keepdims=True)
        o = jnp.dot(p.astype(BF16), v, preferred_element_type=F32)
        o_ref[0, rows, cols] = (o / denom).astype(o_ref.dtype)

    s_prev = scores(units[0])
    for prev, cur in zip(units[:-1], units[1:]):
        s_cur = scores(cur)
        finish(prev, s_prev)
        s_prev = s_cur
    finish(units[-1], s_prev)


def _attention(qkv, batch, seq_len, n_heads, n_pad):
    n_kv = n_heads // GQ
    qkv3 = qkv.reshape(batch, seq_len, (n_heads + 2 * n_kv) * HEAD_DIM)
    k_cols = qkv3[:, :, n_heads * HEAD_DIM:(n_heads + n_kv) * HEAD_DIM]
    kt = k_cols.reshape(batch, seq_len, n_kv, HEAD_DIM).transpose(0, 2, 3, 1)
    tq = _pick(seq_len, (704, 384, 128))
    n_sub = 4
    qw = GQ * HEAD_DIM
    v_off = n_heads + n_kv
    return pl.pallas_call(
        functools.partial(_attn_kernel, n_pad=n_pad, n_sub=n_sub),
        grid=(batch, n_kv, seq_len // tq),
        in_specs=[
            pl.BlockSpec((1, tq, qw), lambda b, h, i: (b, i, h)),
            pl.BlockSpec((1, 1, HEAD_DIM, seq_len), lambda b, h, i: (b, h, 0, 0)),
            pl.BlockSpec((1, seq_len, HEAD_DIM), lambda b, h, i: (b, 0, v_off + h)),
        ],
        out_specs=pl.BlockSpec((1, tq, qw), lambda b, h, i: (b, i, h)),
        out_shape=jax.ShapeDtypeStruct((batch, seq_len, n_heads * HEAD_DIM), BF16),
        compiler_params=_params("parallel", "parallel", "arbitrary"),
        name="attn_core",
    )(qkv3, kt, qkv3)


def _conv_kernel(u_ref, w_ref, b_ref, o_ref, pad_ref, wb_ref, *, n_pad, halo):
    seq = u_ref.shape[1]
    tc = u_ref.shape[2]
    taps = w_ref.shape[0]
    half = (taps - 1) // 2
    nc = seq // CHUNK
    zeros = jnp.zeros((halo, tc), F32)
    pad_ref[pl.ds(0, halo), :] = zeros
    pad_ref[pl.ds(halo + seq, halo), :] = zeros

    def valid_rows(c):
        rows = c * CHUNK + lax.broadcasted_iota(jnp.int32, (CHUNK, 1), 0)
        return (rows >= n_pad).astype(F32)

    def fill(c, carry):
        r0 = pl.multiple_of(c * CHUNK, CHUNK)
        pad_ref[pl.ds(halo + r0, CHUNK), :] = u_ref[0, pl.ds(r0, CHUNK), :].astype(F32) * valid_rows(c)
        return carry

    lax.fori_loop(0, nc, fill, 0)

    sub = lax.broadcasted_iota(jnp.int32, (SUBLANES, 1), 0)
    groups = CHUNK // SUBLANES
    for t in range(taps):
        wb_ref[t] = jnp.broadcast_to(w_ref[pl.ds(t, 1), :], (SUBLANES, tc))
    wb_ref[taps] = jnp.broadcast_to(b_ref[...], (SUBLANES, tc))

    def conv(c, carry):
        r0 = pl.multiple_of(c * CHUNK, CHUNK)
        tiles = [pad_ref[pl.ds(r0 + SUBLANES * i, SUBLANES), :] for i in range(groups + 2)]
        fwd = {k: [pltpu.roll(t, SUBLANES - k, 0) for t in tiles] for k in range(1, half + 1)}
        bwd = {k: [pltpu.roll(t, k, 0) for t in tiles] for k in range(1, half + 1)}
        outs = []
        for i in range(groups):
            acc = wb_ref[taps] + wb_ref[half] * tiles[i + 1]
            for k in range(1, half + 1):
                ahead = jnp.where(sub < SUBLANES - k, fwd[k][i + 1], fwd[k][i + 2])
                behind = jnp.where(sub >= k, bwd[k][i + 1], bwd[k][i])
                acc = acc + wb_ref[half + k] * ahead + wb_ref[half - k] * behind
            outs.append(acc)
        acc = jnp.concatenate(outs, axis=0)
        o_ref[0, pl.ds(r0, CHUNK), :] = (_silu(acc) * valid_rows(c)).astype(o_ref.dtype)
        return carry

    lax.fori_loop(0, nc, conv, 0)


def _ssd_conv(zx3, conv_w, conv_b, d_inner, n_pad):
    batch, seq_len, _ = zx3.shape
    taps, ch = conv_w.shape
    tc = _pick(ch, (512, 256, 128))
    assert d_inner % tc == 0
    col0 = d_inner // tc
    halo = SUBLANES
    assert (taps - 1) // 2 < SUBLANES
    return pl.pallas_call(
        functools.partial(_conv_kernel, n_pad=n_pad, halo=halo),
        grid=(batch, ch // tc),
        in_specs=[
            pl.BlockSpec((1, seq_len, tc), lambda b, j: (b, 0, col0 + j)),
            pl.BlockSpec((taps, tc), lambda b, j: (0, j)),
            pl.BlockSpec((1, tc), lambda b, j: (0, j)),
        ],
        out_specs=pl.BlockSpec((1, seq_len, tc), lambda b, j: (b, 0, j)),
        out_shape=jax.ShapeDtypeStruct((batch, seq_len, ch), BF16),
        scratch_shapes=[pltpu.VMEM((seq_len + 2 * halo, tc), F32),
                        pltpu.VMEM((taps + 1, SUBLANES, tc), F32)],
        compiler_params=_params("parallel", "parallel"),
        name="ssd_conv",
    )(zx3, conv_w, conv_b.reshape(1, ch))


def _softplus(x):
    return jnp.maximum(x, 0.0) + jnp.log(1.0 + jnp.exp(-jnp.abs(x)))


def _ssd_kernel(xs_ref, b_ref, c_ref, z_ref, dtr_ref, pc_ref, dskip_ref, nw_ref, g_ref, y_ref, st_ref, xw_ref,
                tabt_ref, rows_ref, *, n_pad, hpg):
    seq = xs_ref.shape[1]
    gw = xs_ref.shape[2]
    nc = seq // CHUNK
    nh = 2 * hpg
    pw = 2 * SSD_HEAD_DIM
    hi = lax.Precision.HIGHEST

    row_i = lax.broadcasted_iota(jnp.int32, (CHUNK, CHUNK), 0)
    col_i = lax.broadcasted_iota(jnp.int32, (CHUNK, CHUNK), 1)
    lower = row_i >= col_i
    upper = row_i <= col_i
    eye = row_i == col_i
    lower_f = lower.astype(F32)
    upper_f = upper.astype(F32)
    first_of_pair = lax.broadcasted_iota(jnp.int32, (1, pw), 1) < SSD_HEAD_DIM
    tab_row = lax.broadcasted_iota(jnp.int32, (CHUNK, gw), 0)
    lane_head = lax.shift_right_logical(lax.broadcasted_iota(jnp.int32, (CHUNK, gw), 1),
                                        SSD_HEAD_DIM.bit_length() - 1)
    expand = [(tab_row == nh + d * hpg + lane_head).astype(BF16) for d in range(2)]

    bias_col = pc_ref[0, :, pl.ds(0, 1)]
    aneg2_col = -jnp.exp(pc_ref[0, :, pl.ds(1, 1)]) * LOG2E
    dskip = dskip_ref[0]

    def tables_head(c, d):
        lanes = c * CHUNK + lax.broadcasted_iota(jnp.int32, (1, CHUNK), 1)
        dt_row = _softplus(dtr_ref[0, 0, c] + bias_col) * (lanes >= n_pad).astype(F32)
        a_row = dt_row * aneg2_col
        log2_dt = jnp.log2(dt_row)
        tri_f = upper_f if d == 0 else lower_f
        cs_row = jnp.dot(a_row, tri_f, precision=hi, preferred_element_type=F32)
        total = jnp.sum(a_row, axis=1, keepdims=True)
        upd_row = (total - cs_row) + log2_dt
        src_row = cs_row - log2_dt
        table_t = jnp.concatenate(
            [cs_row, upd_row, jnp.zeros((CHUNK - 2 * nh, CHUNK), F32)], axis=0).T
        return cs_row, src_row, jnp.broadcast_to(jnp.exp2(total), (nh, CHUNK)), table_t

    def tables_tail(c, d, slot, tables):
        cs_row, src_row, decay_rows, table_t = tables
        r0 = pl.multiple_of(c * CHUNK, CHUNK)
        w_exp = jnp.dot(jnp.exp2(table_t).astype(BF16), expand[d], preferred_element_type=F32)
        x = xs_ref[0, pl.ds(r0, CHUNK), :]
        xw_ref[slot, d] = (x.astype(F32) * w_exp).astype(BF16)
        tabt_ref[slot, d] = table_t
        rows_ref[slot, d, pl.ds(0, nh), :] = cs_row
        rows_ref[slot, d, pl.ds(nh, nh), :] = src_row
        rows_ref[slot, d, pl.ds(2 * nh, nh), :] = decay_rows

    def scan_head(c, d, slot):
        r0 = pl.multiple_of(c * CHUNK, CHUNK)
        x = xs_ref[0, pl.ds(r0, CHUNK), :]
        bm = b_ref[0, pl.ds(r0, CHUNK), :]
        cm = c_ref[0, pl.ds(r0, CHUNK), :]
        cb = lax.dot_general(cm, bm, (((1,), (1,)), ((), ())), preferred_element_type=F32)
        bt = bm.astype(F32).T.astype(BF16)
        upd = jnp.dot(bt, xw_ref[slot, d], preferred_element_type=F32)
        state = st_ref[d]
        z = jnp.dot(cm, state.astype(BF16), preferred_element_type=F32)
        return r0, x, cb, upd, state, z

    def scan_tail(d, slot, head):
        r0, x, cb, upd, state, z = head
        tri = lower if d == 0 else upper
        table_t = tabt_ref[slot, d]
        cs_row = rows_ref[slot, d, pl.ds(0, nh), :]
        src_row = rows_ref[slot, d, pl.ds(nh, nh), :]
        chunk_decay = rows_ref[slot, d, pl.ds(2 * nh, nh), :]
        xz = jnp.concatenate([x, z.astype(BF16)], axis=0)
        ys = []
        new_states = []
        for pair in range(hpg // 2):
            cols = slice(pair * pw, (pair + 1) * pw)
            rhs = xz[:, cols]
            h0 = d * hpg + 2 * pair
            outs = []
            for hh in (h0, h0 + 1):
                seg = jnp.where(tri, table_t[:, hh:hh + 1] - src_row[hh:hh + 1, :], -jnp.inf)
                m_diag = cb * jnp.exp2(seg)
                carry_in = jnp.where(eye, jnp.exp2(cs_row[hh:hh + 1, :]), 0.0)
                lhs = jnp.concatenate([m_diag, carry_in], axis=1).astype(BF16)
                outs.append(jnp.dot(lhs, rhs, preferred_element_type=F32))
            ys.append(jnp.where(first_of_pair, outs[0], outs[1]))
            decay = jnp.where(first_of_pair, chunk_decay[h0:h0 + 1, :], chunk_decay[h0 + 1:h0 + 2, :])
            new_states.append(decay * state[:, cols] + upd[:, cols])
        st_ref[d] = jnp.concatenate(new_states, axis=1)
        y = jnp.concatenate(ys, axis=1)
        if d == 0:
            y = y + dskip * x.astype(F32)
        y_ref[pl.ds(r0, CHUNK), :] += y

    def gate_chunk(c):
        r0 = pl.multiple_of(c * CHUNK, CHUNK)
        gated = y_ref[pl.ds(r0, CHUNK), :] * _silu(z_ref[0, pl.ds(r0, CHUNK), :].astype(F32))
        g_ref[0, pl.ds(r0, CHUNK), :] = _rms_rows(gated, nw_ref[0]).astype(g_ref.dtype)

    def step(i, slot, nxt, gate):
        heads = [scan_head(i, 0, slot), scan_head(nc - 1 - i, 1, slot)]
        if nxt is not None:
            tabs = [tables_head(nxt, 0), tables_head(nc - 1 - nxt, 1)]
        scan_tail(0, slot, heads[0])
        scan_tail(1, slot, heads[1])
        if gate:
            gate_chunk(i)
            gate_chunk(nc - 1 - i)
        if nxt is not None:
            tables_tail(nxt, 0, 1 - slot, tabs[0])
            tables_tail(nc - 1 - nxt, 1, 1 - slot, tabs[1])

    y_ref[...] = jnp.zeros(y_ref.shape, F32)
    st_ref[...] = jnp.zeros(st_ref.shape, F32)
    tables_tail(0, 0, 0, tables_head(0, 0))
    tables_tail(nc - 1, 1, 0, tables_head(nc - 1, 1))

    def two_steps(gate_first, gate_second):
        def body(k, carry):
            i = 2 * k
            step(i, 0, i + 1, gate_first)
            step(i + 1, 1, jnp.minimum(i + 2, nc - 1), gate_second)
            return carry
        return body

    first_gated = nc // 2
    plain_iters = first_gated // 2
    lax.fori_loop(0, plain_iters, two_steps(False, False), 0)
    gated_from = plain_iters
    if first_gated % 2:
        two_steps(False, True)(plain_iters, 0)
        gated_from += 1
    lax.fori_loop(gated_from, nc // 2, two_steps(True, True), 0)
    if nc % 2:
        step(nc - 1, 0, None, True)


def _ssd_core(xbc, zx3, dt_raw, dt_bias, a_log, d_skip, norm_w, batch, seq_len, d_inner, n_pad):
    groups = SSD_GROUPS
    heads = d_inner // SSD_HEAD_DIM
    hpg = heads // groups
    gw = hpg * SSD_HEAD_DIM
    nc = seq_len // CHUNK
    assert gw % (2 * SSD_HEAD_DIM) == 0 and gw % LANES == 0
    assert gw * groups == d_inner
    b_off = d_inner // SSD_STATE
    c_off = b_off + groups
    dt_rowform = dt_raw.reshape(batch, nc, CHUNK, 2, groups, hpg).transpose(0, 4, 1, 3, 5, 2)
    dt_rowform = dt_rowform.reshape(batch, groups, nc, 2 * hpg, CHUNK)
    per_head = jnp.stack([dt_bias.reshape(2, groups, hpg), a_log.reshape(2, groups, hpg)], -1)
    p_col = per_head.transpose(1, 0, 2, 3).reshape(groups, 2 * hpg, 2).astype(F32)
    d_exp = jnp.repeat(d_skip.astype(F32), SSD_HEAD_DIM).reshape(groups, 1, gw)
    return pl.pallas_call(
        functools.partial(_ssd_kernel, n_pad=n_pad, hpg=hpg),
        grid=(batch, groups),
        in_specs=[
            pl.BlockSpec((1, seq_len, gw), lambda b, g: (b, 0, g)),
            pl.BlockSpec((1, seq_len, SSD_STATE), lambda b, g: (b, 0, b_off + g)),
            pl.BlockSpec((1, seq_len, SSD_STATE), lambda b, g: (b, 0, c_off + g)),
            pl.BlockSpec((1, seq_len, gw), lambda b, g: (b, 0, g)),
            pl.BlockSpec((1, 1, nc, 2 * hpg, CHUNK), lambda b, g: (b, g, 0, 0, 0)),
            pl.BlockSpec((1, 2 * hpg, 2), lambda b, g: (g, 0, 0)),
            pl.BlockSpec((1, 1, gw), lambda b, g: (g, 0, 0)),
            pl.BlockSpec((1, 1, gw), lambda b, g: (g, 0, 0)),
        ],
        out_specs=pl.BlockSpec((1, seq_len, gw), lambda b, g: (b, 0, g)),
        out_shape=jax.ShapeDtypeStruct((batch, seq_len, d_inner), BF16),
        scratch_shapes=[
            pltpu.VMEM((seq_len, gw), F32),
            pltpu.VMEM((2, SSD_STATE, gw), F32),
            pltpu.VMEM((2, 2, CHUNK, gw), BF16),
            pltpu.VMEM((2, 2, CHUNK, CHUNK), F32),
            pltpu.VMEM((2, 2, 6 * hpg, CHUNK), F32),
        ],
        compiler_params=_params("parallel", "parallel"),
        name="ssd_core",
    )(xbc, xbc, xbc, zx3, dt_rowform, p_col, d_exp, norm_w.astype(F32).reshape(groups, 1, gw))


def _ssd_mixer(h, batch, seq_len, n_pad, mix_norm, w_zx, w_dt, out_proj, conv_w, conv_b, dt_bias, a_log,
               d_skip, norm_w):
    d_inner = out_proj.shape[0]
    n_zx = w_zx.shape[1]
    zx = _norm_matmul(h, mix_norm, w_zx, BF16, "ssd_in_zx")
    dt_raw = _norm_matmul(h, mix_norm, w_dt, F32, "ssd_in_dt")
    zx3 = zx.reshape(batch, seq_len, n_zx)
    xbc = _ssd_conv(zx3, conv_w, conv_b, d_inner, n_pad)
    g = _ssd_core(xbc, zx3, dt_raw, dt_bias, a_log, d_skip, norm_w, batch, seq_len, d_inner, n_pad)
    return _matmul_res(g.reshape(batch * seq_len, d_inner), out_proj, h, "ssd_out")


def _rope_tables(seq_len, n_pad, n_meta):
    n_tok = seq_len - n_pad - n_meta
    rows_n = n_tok // GRID_W
    row = jnp.concatenate([jnp.zeros((n_pad,), jnp.int32), jnp.full((n_meta,), -1, jnp.int32),
                           jnp.repeat(jnp.arange(rows_n, dtype=jnp.int32), GRID_W)])
    col = jnp.concatenate([jnp.zeros((n_pad,), jnp.int32), jnp.arange(n_meta, dtype=jnp.int32),
                           jnp.tile(jnp.arange(GRID_W, dtype=jnp.int32), rows_n)])
    inv_freq = ROPE_THETA ** (-jnp.arange(0, 2 * ROPE_HALF, 2, dtype=F32) / (2 * ROPE_HALF))
    ang_r = row.astype(F32)[:, None] * inv_freq
    ang_c = col.astype(F32)[:, None] * inv_freq
    cos = jnp.concatenate([jnp.cos(ang_r), jnp.cos(ang_r), jnp.cos(ang_c), jnp.cos(ang_c)], -1)
    sin = jnp.concatenate([-jnp.sin(ang_r), jnp.sin(ang_r), -jnp.sin(ang_c), jnp.sin(ang_c)], -1)
    return cos, sin


def _attention_mixer(h, batch, seq_len, n_pad, cos, sin, mix_norm, w_qkv, w_o, q_norm, k_norm):
    n_heads = w_o.shape[0] // HEAD_DIM
    n_kv = n_heads // GQ
    head_w = jnp.concatenate([jnp.tile(q_norm.astype(F32) * (HEAD_DIM ** -0.5 * LOG2E), n_heads),
                              jnp.tile(k_norm.astype(F32), n_kv),
                              jnp.ones((n_kv * HEAD_DIM,), F32)]).reshape(1, -1)
    qkv = _qkv_proj(h, mix_norm, w_qkv, head_w, cos, sin, seq_len, (n_heads + n_kv) * HEAD_DIM)
    o = _attention(qkv, batch, seq_len, n_heads, n_pad)
    return _matmul_res(o.reshape(batch * seq_len, n_heads * HEAD_DIM), w_o, h, "attn_out")


def kernel(x, meta_tokens, ffn_norm, ffn_w_in, ffn_w_out, mix_norm, ssd_in_proj, ssd_conv_w, ssd_conv_b,
           ssd_dt_bias, ssd_A_log, ssd_D, ssd_norm, ssd_out_proj, attn_w_qkv, attn_q_norm, attn_k_norm,
           attn_w_o):
    batch, n_tok, d = x.shape
    n_meta = meta_tokens.shape[0]
    n_pad = CHUNK - n_meta
    seq_len = n_pad + n_meta + n_tok
    depth = ffn_norm.shape[0]
    cos, sin = _rope_tables(seq_len, n_pad, n_meta)
    h = jnp.concatenate([jnp.zeros((batch, n_pad, d), x.dtype),
                         jnp.broadcast_to(meta_tokens.astype(x.dtype)[None], (batch, n_meta, d)),
                         x], axis=1).reshape(batch * seq_len, d)
    n_dt = 2 * (ssd_out_proj.shape[1] // SSD_HEAD_DIM)
    n_zx = ssd_in_proj.shape[2] - n_dt
    w_in, w_out = ffn_w_in[0, 0].astype(BF16), ffn_w_out[0, 0].astype(BF16)
    for i in range(depth):
        j = i // 2
        if i % 2 == 0:
            mixer_casts = [_Cast(ssd_in_proj, (j,), 0, n_zx), _Cast(ssd_in_proj, (j,), n_zx, n_dt),
                           _Cast(ssd_out_proj, (j,))]
        else:
            mixer_casts = [_Cast(attn_w_qkv, (j,)), _Cast(attn_w_o, (j,))]
        h, cast = _ffn(h, ffn_norm[i, 0], w_in, w_out,
                       mixer_casts + [_Cast(ffn_w_in, (i, 1)), _Cast(ffn_w_out, (i, 1))])
        *mixer_w, w_in, w_out = cast
        if i % 2 == 0:
            h = _ssd_mixer(h, batch, seq_len, n_pad, mix_norm[i], *mixer_w, ssd_conv_w[j], ssd_conv_b[j],
                           ssd_dt_bias[j], ssd_A_log[j], ssd_D[j], ssd_norm[j])
        else:
            h = _attention_mixer(h, batch, seq_len, n_pad, cos, sin, mix_norm[i], *mixer_w,
                                 attn_q_norm[j], attn_k_norm[j])
        last = i + 1 == depth
        h, cast = _ffn(h, ffn_norm[i, 1], w_in, w_out,
                       [] if last else [_Cast(ffn_w_in, (i + 1, 0)), _Cast(ffn_w_out, (i + 1, 0))])
        if not last:
            w_in, w_out = cast
    return h.reshape(batch, seq_len, d)[:, n_pad + n_meta:, :]
```

```python
import functools

import jax
import jax.numpy as jnp
from jax import lax
from jax.experimental import pallas as pl
from jax.experimental.pallas import tpu as pltpu

F32 = jnp.float32
BF16 = jnp.bfloat16

GRID_W = 64
CHUNK = 128
EPS = 1e-6
SSD_HEAD_DIM = 64
SSD_GROUPS = 8
SSD_STATE = 128
HEAD_DIM = 128
GQ = 2
ROPE_THETA = 10000.0
ROPE_HALF = HEAD_DIM // 4
LOG2E = 1.4426950408889634

V7X_VMEM_LIMIT_BYTES = 58 * 1024 * 1024
FFN_VMEM_BUDGET_BYTES = 56 * 1024 * 1024
FFN_TEMP_BYTES = 3 * 1024 * 1024
LANES = 128
SUBLANES = 8
BF16_TILE_ROWS = 16


def _params(*sem):
    return pltpu.CompilerParams(dimension_semantics=sem, vmem_limit_bytes=V7X_VMEM_LIMIT_BYTES)


def _pick(n, candidates):
    for c in candidates:
        if n % c == 0:
            return c
    return n


def _rms_rows(x, w):
    ms = jnp.mean(x * x, axis=-1, keepdims=True)
    return x * lax.rsqrt(ms + EPS) * w


def _silu(x):
    return x / (1.0 + jnp.exp(-x))


def _ffn_kernel(x_ref, nw_ref, wg_ref, wu_ref, wo_ref, *refs, n_casts):
    cast_in, o_ref, cast_out, hn_ref = refs[:n_casts], refs[n_casts], refs[n_casts + 1:-1], refs[-1]
    j = pl.program_id(1)

    @pl.when(j == 0)
    def _():
        x = x_ref[...]
        hn_ref[...] = _rms_rows(x, nw_ref[...]).astype(BF16)
        o_ref[...] = x

    hn = hn_ref[...]
    g = jnp.dot(hn, wg_ref[...], preferred_element_type=F32)
    u = jnp.dot(hn, wu_ref[...], preferred_element_type=F32)
    a = (0.5 * _silu(g) * u).astype(BF16)
    o_ref[...] += jnp.dot(a, wo_ref[...], preferred_element_type=F32)
    for src, dst in zip(cast_in, cast_out):
        dst[...] = src[...].astype(BF16)


class _Cast:
    def __init__(self, array, lead, col0=0, cols=None):
        self.array, self.lead, self.col0 = array, tuple(lead), col0
        self.rows = array.shape[-2]
        self.cols = array.shape[-1] - col0 if cols is None else cols
        assert col0 % self.cols == 0

    def specs(self, n_steps, step_of):
        nb = 1
        while 2 * nb <= n_steps and self.rows % (2 * nb) == 0 and (self.rows // (2 * nb)) % BF16_TILE_ROWS == 0:
            nb *= 2
        rb = self.rows // nb
        lead, cb = self.lead, self.col0 // self.cols

        def block(*grid_idx):
            return jnp.minimum(step_of(*grid_idx), nb - 1)

        in_spec = pl.BlockSpec((None,) * len(lead) + (rb, self.cols), lambda *g: lead + (block(*g), cb))
        out_spec = pl.BlockSpec((rb, self.cols), lambda *g: (block(*g), 0))
        return in_spec, out_spec, jax.ShapeDtypeStruct((self.rows, self.cols), BF16)


def _ffn(h, norm_w, w_in, w_out, casts=()):
    m, d = h.shape
    f = w_out.shape[0]
    tm = _pick(m, (768, 384, 256, 128))
    for tf in (768, 512, 256, 128):
        if f % tf:
            continue
        nf = f // tf
        grid = (m // tm, nf)
        cast_specs = [c.specs(grid[0] * grid[1], lambda i, j, nf=nf: i * nf + j) for c in casts]
        cast_bytes = sum(cs[1].block_shape[0] * cs[1].block_shape[1] * (4 + 2) for cs in cast_specs)
        need = 2 * (2 * tm * d * 4 + 3 * d * tf * 2 + cast_bytes) + tm * d * 2 + FFN_TEMP_BYTES
        if need <= FFN_VMEM_BUDGET_BYTES:
            break
    outs = pl.pallas_call(
        functools.partial(_ffn_kernel, n_casts=len(casts)),
        grid=grid,
        in_specs=[
            pl.BlockSpec((tm, d), lambda i, j: (i, 0)),
            pl.BlockSpec((1, d), lambda i, j: (0, 0)),
            pl.BlockSpec((d, tf), lambda i, j: (0, j)),
            pl.BlockSpec((d, tf), lambda i, j: (0, j + nf)),
            pl.BlockSpec((tf, d), lambda i, j: (j, 0)),
        ] + [cs[0] for cs in cast_specs],
        out_specs=[pl.BlockSpec((tm, d), lambda i, j: (i, 0))] + [cs[1] for cs in cast_specs],
        out_shape=[jax.ShapeDtypeStruct((m, d), F32)] + [cs[2] for cs in cast_specs],
        scratch_shapes=[pltpu.VMEM((tm, d), BF16)],
        compiler_params=_params("arbitrary", "arbitrary"),
        name="ffn",
    )(h, norm_w.reshape(1, d), w_in, w_in, w_out, *[c.array for c in casts])
    return outs[0], list(outs[1:])


def _norm_matmul_kernel(x_ref, nw_ref, w_ref, o_ref, hn_ref):
    @pl.when(pl.program_id(1) == 0)
    def _():
        hn_ref[...] = _rms_rows(x_ref[...], nw_ref[...]).astype(BF16)

    o_ref[...] = jnp.dot(hn_ref[...], w_ref[...], preferred_element_type=F32).astype(o_ref.dtype)


def _norm_matmul(h, norm_w, w, out_dtype, name):
    m, d = h.shape
    n = w.shape[1]
    tm = _pick(m, (1408, 768, 384, 256, 128))
    tn = _pick(n, (1024, 512, 256, 128))
    return pl.pallas_call(
        _norm_matmul_kernel,
        grid=(m // tm, n // tn),
        in_specs=[
            pl.BlockSpec((tm, d), lambda i, j: (i, 0)),
            pl.BlockSpec((1, d), lambda i, j: (0, 0)),
            pl.BlockSpec((d, tn), lambda i, j: (0, j)),
        ],
        out_specs=pl.BlockSpec((tm, tn), lambda i, j: (i, j)),
        out_shape=jax.ShapeDtypeStruct((m, n), out_dtype),
        scratch_shapes=[pltpu.VMEM((tm, d), BF16)],
        compiler_params=_params("parallel", "arbitrary"),
        name=name,
    )(h, norm_w.reshape(1, d), w)


def _matmul_res_kernel(a_ref, w_ref, r_ref, o_ref):
    o_ref[...] = r_ref[...] + jnp.dot(a_ref[...], w_ref[...], preferred_element_type=F32)


def _matmul_res(a, w, res, name):
    m, k = a.shape
    n = w.shape[1]
    tm = _pick(m, (768, 384, 256, 128))
    tn = _pick(n, (1024, 512, 256, 128))
    return pl.pallas_call(
        _matmul_res_kernel,
        grid=(m // tm, n // tn),
        in_specs=[
            pl.BlockSpec((tm, k), lambda i, j: (i, 0)),
            pl.BlockSpec((k, tn), lambda i, j: (0, j)),
            pl.BlockSpec((tm, tn), lambda i, j: (i, j)),
        ],
        out_specs=pl.BlockSpec((tm, tn), lambda i, j: (i, j)),
        out_shape=jax.ShapeDtypeStruct((m, n), F32),
        compiler_params=_params("parallel", "parallel"),
        name=name,
    )(a, w, res)


def _qkv_kernel(x_ref, nw_ref, w_ref, hw_ref, cos_ref, sin_ref, o_ref, hn_ref, *, n_rope_tiles):
    j = pl.program_id(1)

    @pl.when(j == 0)
    def _():
        hn_ref[...] = _rms_rows(x_ref[...], nw_ref[...]).astype(BF16)

    tn = w_ref.shape[1]
    pw = 2 * HEAD_DIM

    @pl.when(j < n_rope_tiles)
    def _():
        acc = jnp.dot(hn_ref[...], w_ref[...], preferred_element_type=F32)
        c = jnp.concatenate([cos_ref[...]] * 2, axis=1)
        s = jnp.concatenate([sin_ref[...]] * 2, axis=1)
        src = lax.broadcasted_iota(jnp.int32, (pw, pw), 0)
        dst = lax.broadcasted_iota(jnp.int32, (pw, pw), 1)
        same_head = (src // HEAD_DIM == dst // HEAD_DIM).astype(BF16)
        partner_of = jnp.where(dst % (2 * ROPE_HALF) < ROPE_HALF, dst + ROPE_HALF, dst - ROPE_HALF)
        swap = (src == partner_of).astype(BF16)
        for p0 in range(0, tn, pw):
            sl = slice(p0, p0 + pw)
            y = acc[:, sl]
            sumsq = jnp.dot((y * y).astype(BF16), same_head, preferred_element_type=F32)
            a = y * hw_ref[:, sl]
            partner = jnp.dot(a.astype(BF16), swap, preferred_element_type=F32)
            scale = lax.rsqrt(sumsq * (1.0 / HEAD_DIM) + EPS)
            o_ref[:, sl] = ((a * c + partner * s) * scale).astype(o_ref.dtype)

    @pl.when(j >= n_rope_tiles)
    def _():
        o_ref[...] = jnp.dot(hn_ref[...], w_ref[...], preferred_element_type=F32).astype(o_ref.dtype)


def _qkv_proj(h, norm_w, w, head_w, cos, sin, seq_len, n_rope_cols):
    m, d = h.shape
    n = w.shape[1]
    tm = _pick(seq_len, (1408, 384, 128))
    tn = _pick(n_rope_cols, (512, 256))
    assert n % tn == 0 and n_rope_cols % tn == 0
    tiles_per_seq = seq_len // tm
    kern = functools.partial(_qkv_kernel, n_rope_tiles=n_rope_cols // tn)
    return pl.pallas_call(
        kern,
        grid=(m // tm, n // tn),
        in_specs=[
            pl.BlockSpec((tm, d), lambda i, j: (i, 0)),
            pl.BlockSpec((1, d), lambda i, j: (0, 0)),
            pl.BlockSpec((d, tn), lambda i, j: (0, j)),
            pl.BlockSpec((1, tn), lambda i, j: (0, j)),
            pl.BlockSpec((tm, HEAD_DIM), lambda i, j: (i % tiles_per_seq, 0)),
            pl.BlockSpec((tm, HEAD_DIM), lambda i, j: (i % tiles_per_seq, 0)),
        ],
        out_specs=pl.BlockSpec((tm, tn), lambda i, j: (i, j)),
        out_shape=jax.ShapeDtypeStruct((m, n), BF16),
        scratch_shapes=[pltpu.VMEM((tm, d), BF16)],
        compiler_params=_params("parallel", "arbitrary"),
        name="attn_qkv",
    )(h, norm_w.reshape(1, d), w, head_w, cos, sin)


def _attn_kernel(q_ref, kt_ref, v_ref, o_ref, *, n_pad, n_sub):
    kt = kt_ref[0, 0]
    v = v_ref[0]
    rs = q_ref.shape[1] // n_sub
    key_valid = lax.broadcasted_iota(jnp.int32, (1, LANES), 1) >= n_pad
    assert n_pad <= LANES
    units = [(slice(r * rs, (r + 1) * rs), slice(g * HEAD_DIM, (g + 1) * HEAD_DIM))
             for g in range(GQ) for r in range(n_sub)]

    def scores(unit):
        rows, cols = unit
        return jnp.dot(q_ref[0, rows, cols], kt, preferred_element_type=F32)

    def finish(unit, s):
        rows, cols = unit
        s = jnp.concatenate([jnp.where(key_valid, s[:, :LANES], -jnp.inf), s[:, LANES:]], axis=1)
        p = jnp.exp2(s - jnp.max(s, axis=-1, keepdims=True))
        denom = jnp.sum(p, axis=-1, keepdims=True)
        o = jnp.dot(p.astype(BF16), v, preferred_element_type=F32)
        o_ref[0, rows, cols] = (o / denom).astype(o_ref.dtype)

    s_prev = scores(units[0])
    for prev, cur in zip(units[:-1], units[1:]):
        s_cur = scores(cur)
        finish(prev, s_prev)
        s_prev = s_cur
    finish(units[-1], s_prev)


def _attention(qkv, batch, seq_len, n_heads, n_pad):
    n_kv = n_heads // GQ
    qkv3 = qkv.reshape(batch, seq_len, (n_heads + 2 * n_kv) * HEAD_DIM)
    k_cols = qkv3[:, :, n_heads * HEAD_DIM:(n_heads + n_kv) * HEAD_DIM]
    kt = k_cols.reshape(batch, seq_len, n_kv, HEAD_DIM).transpose(0, 2, 3, 1)
    tq = _pick(seq_len, (1408, 384, 128))
    n_sub = 8 if tq % (8 * BF16_TILE_ROWS) == 0 else 4
    qw = GQ * HEAD_DIM
    v_off = n_heads + n_kv
    return pl.pallas_call(
        functools.partial(_attn_kernel, n_pad=n_pad, n_sub=n_sub),
        grid=(batch, n_kv, seq_len // tq),
        in_specs=[
            pl.BlockSpec((1, tq, qw), lambda b, h, i: (b, i, h)),
            pl.BlockSpec((1, 1, HEAD_DIM, seq_len), lambda b, h, i: (b, h, 0, 0)),
            pl.BlockSpec((1, seq_len, HEAD_DIM), lambda b, h, i: (b, 0, v_off + h)),
        ],
        out_specs=pl.BlockSpec((1, tq, qw), lambda b, h, i: (b, i, h)),
        out_shape=jax.ShapeDtypeStruct((batch, seq_len, n_heads * HEAD_DIM), BF16),
        compiler_params=_params("parallel", "parallel", "arbitrary"),
        name="attn_core",
    )(qkv3, kt, qkv3)


def _conv_kernel(u_ref, w_ref, b_ref, o_ref, pad_ref, wb_ref, *, n_pad, halo):
    seq = u_ref.shape[1]
    tc = u_ref.shape[2]
    taps = w_ref.shape[0]
    half = (taps - 1) // 2
    nc = seq // CHUNK
    zeros = jnp.zeros((halo, tc), F32)
    pad_ref[pl.ds(0, halo), :] = zeros
    pad_ref[pl.ds(halo + seq, halo), :] = zeros

    def valid_rows(c):
        rows = c * CHUNK + lax.broadcasted_iota(jnp.int32, (CHUNK, 1), 0)
        return (rows >= n_pad).astype(F32)

    def fill(c, carry):
        r0 = pl.multiple_of(c * CHUNK, CHUNK)
        pad_ref[pl.ds(halo + r0, CHUNK), :] = u_ref[0, pl.ds(r0, CHUNK), :].astype(F32) * valid_rows(c)
        return carry

    lax.fori_loop(0, nc, fill, 0)

    win = CHUNK + 2 * halo
    k_pad = -(taps * win) % LANES
    for t in range(taps):
        wb_ref[t] = jnp.broadcast_to(w_ref[pl.ds(t, 1), :], (SUBLANES, tc))
    out_row = lax.broadcasted_iota(jnp.int32, (CHUNK, taps * win + k_pad), 0)
    col = lax.broadcasted_iota(jnp.int32, (CHUNK, taps * win + k_pad), 1)
    tap = sum((col >= m * win).astype(jnp.int32) for m in range(1, taps + 1))
    shift_sum = ((col - tap * win == out_row + (halo - half) + tap) & (tap < taps)).astype(BF16)
    bias = b_ref[...]

    def conv(c):
        r0 = pl.multiple_of(c * CHUNK, CHUNK)
        window = pad_ref[pl.ds(r0, win), :].reshape(win // SUBLANES, SUBLANES, tc)
        weighted = [(window * wb_ref[t]).reshape(win, tc).astype(BF16) for t in range(taps)]
        if k_pad:
            weighted.append(jnp.zeros((k_pad, tc), BF16))
        acc = jnp.dot(shift_sum, jnp.concatenate(weighted, axis=0), preferred_element_type=F32) + bias
        o_ref[0, pl.ds(r0, CHUNK), :] = (_silu(acc) * valid_rows(c)).astype(o_ref.dtype)

    per_body = 3 if nc % 3 == 0 else 2 if nc % 2 == 0 else 1

    def body(k, carry):
        for u in range(per_body):
            conv(k * per_body + u)
        return carry

    lax.fori_loop(0, nc // per_body, body, 0)


def _ssd_conv(zx3, conv_w, conv_b, d_inner, n_pad):
    batch, seq_len, _ = zx3.shape
    taps, ch = conv_w.shape
    tc = _pick(ch, (512, 256, 128))
    assert d_inner % tc == 0
    col0 = d_inner // tc
    halo = SUBLANES
    assert (taps - 1) // 2 < SUBLANES
    return pl.pallas_call(
        functools.partial(_conv_kernel, n_pad=n_pad, halo=halo),
        grid=(batch, ch // tc),
        in_specs=[
            pl.BlockSpec((1, seq_len, tc), lambda b, j: (b, 0, col0 + j)),
            pl.BlockSpec((taps, tc), lambda b, j: (0, j)),
            pl.BlockSpec((1, tc), lambda b, j: (0, j)),
        ],
        out_specs=pl.BlockSpec((1, seq_len, tc), lambda b, j: (b, 0, j)),
        out_shape=jax.ShapeDtypeStruct((batch, seq_len, ch), BF16),
        scratch_shapes=[pltpu.VMEM((seq_len + 2 * halo, tc), F32),
                        pltpu.VMEM((taps, SUBLANES, tc), F32)],
        compiler_params=_params("parallel", "parallel"),
        name="ssd_conv",
    )(zx3, conv_w, conv_b.reshape(1, ch))


def _softplus(x):
    return jnp.maximum(x, 0.0) + jnp.log(1.0 + jnp.exp(-jnp.abs(x)))


def _ssd_kernel(xs_ref, b_ref, c_ref, z_ref, dtr_ref, pc_ref, dskip_ref, nw_ref, g_ref, y_ref, st_ref, xw_ref,
                tabt_ref, rows_ref, *, n_pad, hpg):
    seq = xs_ref.shape[1]
    gw = xs_ref.shape[2]
    nc = seq // CHUNK
    nh = 2 * hpg
    pw = 2 * SSD_HEAD_DIM
    hi = lax.Precision.HIGHEST

    row_i = lax.broadcasted_iota(jnp.int32, (CHUNK, CHUNK), 0)
    col_i = lax.broadcasted_iota(jnp.int32, (CHUNK, CHUNK), 1)
    lower = row_i >= col_i
    upper = row_i <= col_i
    eye = row_i == col_i
    lower_f = lower.astype(F32)
    upper_f = upper.astype(F32)
    first_of_pair = lax.broadcasted_iota(jnp.int32, (1, pw), 1) < SSD_HEAD_DIM
    tab_row = lax.broadcasted_iota(jnp.int32, (CHUNK, gw), 0)
    lane_head = lax.shift_right_logical(lax.broadcasted_iota(jnp.int32, (CHUNK, gw), 1),
                                        SSD_HEAD_DIM.bit_length() - 1)
    expand = [(tab_row == nh + d * hpg + lane_head).astype(BF16) for d in range(2)]

    bias_col = pc_ref[0, :, pl.ds(0, 1)]
    aneg2_col = -jnp.exp(pc_ref[0, :, pl.ds(1, 1)]) * LOG2E
    dskip = dskip_ref[0]

    def tables_head(c, d):
        lanes = c * CHUNK + lax.broadcasted_iota(jnp.int32, (1, CHUNK), 1)
        dt_row = _softplus(dtr_ref[0, 0, c] + bias_col) * (lanes >= n_pad).astype(F32)
        a_row = dt_row * aneg2_col
        log2_dt = jnp.log2(dt_row)
        tri_f = upper_f if d == 0 else lower_f
        cs_row = jnp.dot(a_row, tri_f, precision=hi, preferred_element_type=F32)
        total = jnp.sum(a_row, axis=1, keepdims=True)
        upd_row = (total - cs_row) + log2_dt
        src_row = cs_row - log2_dt
        table_t = jnp.concatenate(
            [cs_row, upd_row, jnp.zeros((CHUNK - 2 * nh, CHUNK), F32)], axis=0).T
        return cs_row, src_row, jnp.broadcast_to(jnp.exp2(total), (nh, CHUNK)), table_t

    def tables_tail(c, d, slot, tables):
        cs_row, src_row, decay_rows, table_t = tables
        r0 = pl.multiple_of(c * CHUNK, CHUNK)
        w_exp = jnp.dot(jnp.exp2(table_t).astype(BF16), expand[d], preferred_element_type=F32)
        x = xs_ref[0, pl.ds(r0, CHUNK), :]
        xw_ref[slot, d] = (x.astype(F32) * w_exp).astype(BF16)
        tabt_ref[slot, d] = table_t
        rows_ref[slot, d, pl.ds(0, nh), :] = cs_row
        rows_ref[slot, d, pl.ds(nh, nh), :] = src_row
        rows_ref[slot, d, pl.ds(2 * nh, nh), :] = decay_rows

    def scan_head(c, d, slot):
        r0 = pl.multiple_of(c * CHUNK, CHUNK)
        x = xs_ref[0, pl.ds(r0, CHUNK), :]
        bm = b_ref[0, pl.ds(r0, CHUNK), :]
        cm = c_ref[0, pl.ds(r0, CHUNK), :]
        cb = lax.dot_general(cm, bm, (((1,), (1,)), ((), ())), preferred_element_type=F32)
        bt = bm.astype(F32).T.astype(BF16)
        upd = jnp.dot(bt, xw_ref[slot, d], preferred_element_type=F32)
        state = st_ref[d]
        z = jnp.dot(cm, state.astype(BF16), preferred_element_type=F32)
        return r0, x, cb, upd, state, z

    def scan_tail(d, slot, head):
        r0, x, cb, upd, state, z = head
        tri = lower if d == 0 else upper
        table_t = tabt_ref[slot, d]
        cs_row = rows_ref[slot, d, pl.ds(0, nh), :]
        src_row = rows_ref[slot, d, pl.ds(nh, nh), :]
        chunk_decay = rows_ref[slot, d, pl.ds(2 * nh, nh), :]
        xz = jnp.concatenate([x, z.astype(BF16)], axis=0)
        ys = []
        new_states = []
        for pair in range(hpg // 2):
            cols = slice(pair * pw, (pair + 1) * pw)
            rhs = xz[:, cols]
            h0 = d * hpg + 2 * pair
            outs = []
            for hh in (h0, h0 + 1):
                seg = jnp.where(tri, table_t[:, hh:hh + 1] - src_row[hh:hh + 1, :], -jnp.inf)
                m_diag = cb * jnp.exp2(seg)
                carry_in = jnp.where(eye, jnp.exp2(cs_row[hh:hh + 1, :]), 0.0)
                lhs = jnp.concatenate([m_diag, carry_in], axis=1).astype(BF16)
                outs.append(jnp.dot(lhs, rhs, preferred_element_type=F32))
            ys.append(jnp.where(first_of_pair, outs[0], outs[1]))
            decay = jnp.where(first_of_pair, chunk_decay[h0:h0 + 1, :], chunk_decay[h0 + 1:h0 + 2, :])
            new_states.append(decay * state[:, cols] + upd[:, cols])
        st_ref[d] = jnp.concatenate(new_states, axis=1)
        y = jnp.concatenate(ys, axis=1)
        if d == 0:
            y = y + dskip * x.astype(F32)
        y_ref[pl.ds(r0, CHUNK), :] += y

    def gate_chunk(c):
        r0 = pl.multiple_of(c * CHUNK, CHUNK)
        gated = y_ref[pl.ds(r0, CHUNK), :] * _silu(z_ref[0, pl.ds(r0, CHUNK), :].astype(F32))
        g_ref[0, pl.ds(r0, CHUNK), :] = _rms_rows(gated, nw_ref[0]).astype(g_ref.dtype)

    def step(i, slot, nxt, gate):
        heads = [scan_head(i, 0, slot), scan_head(nc - 1 - i, 1, slot)]
        if nxt is not None:
            tabs = [tables_head(nxt, 0), tables_head(nc - 1 - nxt, 1)]
        scan_tail(0, slot, heads[0])
        scan_tail(1, slot, heads[1])
        if gate:
            gate_chunk(i)
            gate_chunk(nc - 1 - i)
        if nxt is not None:
            tables_tail(nxt, 0, 1 - slot, tabs[0])
            tables_tail(nc - 1 - nxt, 1, 1 - slot, tabs[1])

    y_ref[...] = jnp.zeros(y_ref.shape, F32)
    st_ref[...] = jnp.zeros(st_ref.shape, F32)
    tables_tail(0, 0, 0, tables_head(0, 0))
    tables_tail(nc - 1, 1, 0, tables_head(nc - 1, 1))

    def two_steps(gate_first, gate_second):
        def body(k, carry):
            i = 2 * k
            step(i, 0, i + 1, gate_first)
            step(i + 1, 1, jnp.minimum(i + 2, nc - 1), gate_second)
            return carry
        return body

    first_gated = nc // 2
    plain_iters = first_gated // 2
    lax.fori_loop(0, plain_iters, two_steps(False, False), 0)
    gated_from = plain_iters
    if first_gated % 2:
        two_steps(False, True)(plain_iters, 0)
        gated_from += 1
    lax.fori_loop(gated_from, nc // 2, two_steps(True, True), 0)
    if nc % 2:
        step(nc - 1, 0, None, True)


def _ssd_core(xbc, zx3, dt_raw, dt_bias, a_log, d_skip, norm_w, batch, seq_len, d_inner, n_pad):
    groups = SSD_GROUPS
    heads = d_inner // SSD_HEAD_DIM
    hpg = heads // groups
    gw = hpg * SSD_HEAD_DIM
    nc = seq_len // CHUNK
    assert gw % (2 * SSD_HEAD_DIM) == 0 and gw % LANES == 0
    assert gw * groups == d_inner
    b_off = d_inner // SSD_STATE
    c_off = b_off + groups
    dt_rowform = dt_raw.reshape(batch, nc, CHUNK, 2, groups, hpg).transpose(0, 4, 1, 3, 5, 2)
    dt_rowform = dt_rowform.reshape(batch, groups, nc, 2 * hpg, CHUNK)
    per_head = jnp.stack([dt_bias.reshape(2, groups, hpg), a_log.reshape(2, groups, hpg)], -1)
    p_col = per_head.transpose(1, 0, 2, 3).reshape(groups, 2 * hpg, 2).astype(F32)
    d_exp = jnp.repeat(d_skip.astype(F32), SSD_HEAD_DIM).reshape(groups, 1, gw)
    return pl.pallas_call(
        functools.partial(_ssd_kernel, n_pad=n_pad, hpg=hpg),
        grid=(batch, groups),
        in_specs=[
            pl.BlockSpec((1, seq_len, gw), lambda b, g: (b, 0, g)),
            pl.BlockSpec((1, seq_len, SSD_STATE), lambda b, g: (b, 0, b_off + g)),
            pl.BlockSpec((1, seq_len, SSD_STATE), lambda b, g: (b, 0, c_off + g)),
            pl.BlockSpec((1, seq_len, gw), lambda b, g: (b, 0, g)),
            pl.BlockSpec((1, 1, nc, 2 * hpg, CHUNK), lambda b, g: (b, g, 0, 0, 0)),
            pl.BlockSpec((1, 2 * hpg, 2), lambda b, g: (g, 0, 0)),
            pl.BlockSpec((1, 1, gw), lambda b, g: (g, 0, 0)),
            pl.BlockSpec((1, 1, gw), lambda b, g: (g, 0, 0)),
        ],
        out_specs=pl.BlockSpec((1, seq_len, gw), lambda b, g: (b, 0, g)),
        out_shape=jax.ShapeDtypeStruct((batch, seq_len, d_inner), BF16),
        scratch_shapes=[
            pltpu.VMEM((seq_len, gw), F32),
            pltpu.VMEM((2, SSD_STATE, gw), F32),
            pltpu.VMEM((2, 2, CHUNK, gw), BF16),
            pltpu.VMEM((2, 2, CHUNK, CHUNK), F32),
            pltpu.VMEM((2, 2, 6 * hpg, CHUNK), F32),
        ],
        compiler_params=_params("parallel", "parallel"),
        name="ssd_core",
    )(xbc, xbc, xbc, zx3, dt_rowform, p_col, d_exp, norm_w.astype(F32).reshape(groups, 1, gw))


def _ssd_mixer(h, batch, seq_len, n_pad, mix_norm, w_zx, w_dt, out_proj, conv_w, conv_b, dt_bias, a_log,
               d_skip, norm_w):
    d_inner = out_proj.shape[0]
    n_zx = w_zx.shape[1]
    zx = _norm_matmul(h, mix_norm, w_zx, BF16, "ssd_in_zx")
    dt_raw = _norm_matmul(h, mix_norm, w_dt, F32, "ssd_in_dt")
    zx3 = zx.reshape(batch, seq_len, n_zx)
    xbc = _ssd_conv(zx3, conv_w, conv_b, d_inner, n_pad)
    g = _ssd_core(xbc, zx3, dt_raw, dt_bias, a_log, d_skip, norm_w, batch, seq_len, d_inner, n_pad)
    return _matmul_res(g.reshape(batch * seq_len, d_inner), out_proj, h, "ssd_out")


def _rope_tables(seq_len, n_pad, n_meta):
    n_tok = seq_len - n_pad - n_meta
    rows_n = n_tok // GRID_W
    row = jnp.concatenate([jnp.zeros((n_pad,), jnp.int32), jnp.full((n_meta,), -1, jnp.int32),
                           jnp.repeat(jnp.arange(rows_n, dtype=jnp.int32), GRID_W)])
    col = jnp.concatenate([jnp.zeros((n_pad,), jnp.int32), jnp.arange(n_meta, dtype=jnp.int32),
                           jnp.tile(jnp.arange(GRID_W, dtype=jnp.int32), rows_n)])
    inv_freq = ROPE_THETA ** (-jnp.arange(0, 2 * ROPE_HALF, 2, dtype=F32) / (2 * ROPE_HALF))
    ang_r = row.astype(F32)[:, None] * inv_freq
    ang_c = col.astype(F32)[:, None] * inv_freq
    cos = jnp.concatenate([jnp.cos(ang_r), jnp.cos(ang_r), jnp.cos(ang_c), jnp.cos(ang_c)], -1)
    sin = jnp.concatenate([-jnp.sin(ang_r), jnp.sin(ang_r), -jnp.sin(ang_c), jnp.sin(ang_c)], -1)
    return cos, sin


def _attention_mixer(h, batch, seq_len, n_pad, cos, sin, mix_norm, w_qkv, w_o, q_norm, k_norm):
    n_heads = w_o.shape[0] // HEAD_DIM
    n_kv = n_heads // GQ
    head_w = jnp.concatenate([jnp.tile(q_norm.astype(F32) * (HEAD_DIM ** -0.5 * LOG2E), n_heads),
                              jnp.tile(k_norm.astype(F32), n_kv),
                              jnp.ones((n_kv * HEAD_DIM,), F32)]).reshape(1, -1)
    qkv = _qkv_proj(h, mix_norm, w_qkv, head_w, cos, sin, seq_len, (n_heads + n_kv) * HEAD_DIM)
    o = _attention(qkv, batch, seq_len, n_heads, n_pad)
    return _matmul_res(o.reshape(batch * seq_len, n_heads * HEAD_DIM), w_o, h, "attn_out")


def kernel(x, meta_tokens, ffn_norm, ffn_w_in, ffn_w_out, mix_norm, ssd_in_proj, ssd_conv_w, ssd_conv_b,
           ssd_dt_bias, ssd_A_log, ssd_D, ssd_norm, ssd_out_proj, attn_w_qkv, attn_q_norm, attn_k_norm,
           attn_w_o):
    batch, n_tok, d = x.shape
    n_meta = meta_tokens.shape[0]
    n_pad = CHUNK - n_meta
    seq_len = n_pad + n_meta + n_tok
    depth = ffn_norm.shape[0]
    cos, sin = _rope_tables(seq_len, n_pad, n_meta)
    h = jnp.concatenate([jnp.zeros((batch, n_pad, d), x.dtype),
                         jnp.broadcast_to(meta_tokens.astype(x.dtype)[None], (batch, n_meta, d)),
                         x], axis=1).reshape(batch * seq_len, d)
    n_dt = 2 * (ssd_out_proj.shape[1] // SSD_HEAD_DIM)
    n_zx = ssd_in_proj.shape[2] - n_dt
    w_in, w_out = ffn_w_in[0, 0].astype(BF16), ffn_w_out[0, 0].astype(BF16)
    for i in range(depth):
        j = i // 2
        if i % 2 == 0:
            mixer_casts = [_Cast(ssd_in_proj, (j,), 0, n_zx), _Cast(ssd_in_proj, (j,), n_zx, n_dt),
                           _Cast(ssd_out_proj, (j,))]
        else:
            mixer_casts = [_Cast(attn_w_qkv, (j,)), _Cast(attn_w_o, (j,))]
        h, cast = _ffn(h, ffn_norm[i, 0], w_in, w_out,
                       mixer_casts + [_Cast(ffn_w_in, (i, 1)), _Cast(ffn_w_out, (i, 1))])
        *mixer_w, w_in, w_out = cast
        if i % 2 == 0:
            h = _ssd_mixer(h, batch, seq_len, n_pad, mix_norm[i], *mixer_w, ssd_conv_w[j], ssd_conv_b[j],
                           ssd_dt_bias[j], ssd_A_log[j], ssd_D[j], ssd_norm[j])
        else:
            h = _attention_mixer(h, batch, seq_len, n_pad, cos, sin, mix_norm[i], *mixer_w,
                                 attn_q_norm[j], attn_k_norm[j])
        last = i + 1 == depth
        h, cast = _ffn(h, ffn_norm[i, 1], w_in, w_out,
                       [] if last else [_Cast(ffn_w_in, (i + 1, 0)), _Cast(ffn_w_out, (i + 1, 0))])
        if not last:
            w_in, w_out = cast
    return h.reshape(batch, seq_len, d)[:, n_pad + n_meta:, :]
```

```python
import functools

import jax
import jax.numpy as jnp
from jax import lax
from jax.experimental import pallas as pl
from jax.experimental.pallas import tpu as pltpu

F32 = jnp.float32
BF16 = jnp.bfloat16

GRID_W = 64
CHUNK = 128
EPS = 1e-6
SSD_HEAD_DIM = 64
SSD_GROUPS = 8
SSD_STATE = 128
HEAD_DIM = 128
GQ = 2
ROPE_THETA = 10000.0
ROPE_HALF = HEAD_DIM // 4
LOG2E = 1.4426950408889634

V7X_VMEM_LIMIT_BYTES = 58 * 1024 * 1024
FFN_VMEM_BUDGET_BYTES = 56 * 1024 * 1024
FFN_TEMP_BYTES = 3 * 1024 * 1024
LANES = 128
SUBLANES = 8
BF16_TILE_ROWS = 16


def _params(*sem):
    return pltpu.CompilerParams(dimension_semantics=sem, vmem_limit_bytes=V7X_VMEM_LIMIT_BYTES)


def _pick(n, candidates):
    for c in candidates:
        if n % c == 0:
            return c
    return n


def _rms_rows(x, w):
    ms = jnp.mean(x * x, axis=-1, keepdims=True)
    return x * lax.rsqrt(ms + EPS) * w


def _silu(x):
    half = 0.5 * x
    return half + half * jnp.tanh(half)


def _ffn_kernel(x_ref, nw_ref, wg_ref, wu_ref, wo_ref, *refs, n_casts):
    cast_in, o_ref, cast_out, hn_ref = refs[:n_casts], refs[n_casts], refs[n_casts + 1:-1], refs[-1]
    j = pl.program_id(1)

    @pl.when(j == 0)
    def _():
        x = x_ref[...]
        hn_ref[...] = _rms_rows(x, nw_ref[...]).astype(BF16)
        o_ref[...] = x

    hn = hn_ref[...]
    g = jnp.dot(hn, wg_ref[...], preferred_element_type=F32)
    u = jnp.dot(hn, wu_ref[...], preferred_element_type=F32)
    a = (0.5 * _silu(g) * u).astype(BF16)
    o_ref[...] += jnp.dot(a, wo_ref[...], preferred_element_type=F32)
    for src, dst in zip(cast_in, cast_out):
        dst[...] = src[...].astype(BF16)


class _Cast:
    def __init__(self, array, lead, col0=0, cols=None):
        self.array, self.lead, self.col0 = array, tuple(lead), col0
        self.rows = array.shape[-2]
        self.cols = array.shape[-1] - col0 if cols is None else cols
        assert col0 % self.cols == 0

    def specs(self, n_steps, step_of):
        nb = 1
        while 2 * nb <= n_steps and self.rows % (2 * nb) == 0 and (self.rows // (2 * nb)) % BF16_TILE_ROWS == 0:
            nb *= 2
        rb = self.rows // nb
        lead, cb = self.lead, self.col0 // self.cols

        def block(*grid_idx):
            return jnp.minimum(step_of(*grid_idx), nb - 1)

        in_spec = pl.BlockSpec((None,) * len(lead) + (rb, self.cols), lambda *g: lead + (block(*g), cb))
        out_spec = pl.BlockSpec((rb, self.cols), lambda *g: (block(*g), 0))
        return in_spec, out_spec, jax.ShapeDtypeStruct((self.rows, self.cols), BF16)


def _ffn(h, norm_w, w_in, w_out, casts=()):
    m, d = h.shape
    f = w_out.shape[0]
    tm = _pick(m, (768, 384, 256, 128))
    for tf in (768, 512, 256, 128):
        if f % tf:
            continue
        nf = f // tf
        grid = (m // tm, nf)
        cast_specs = [c.specs(grid[0] * grid[1], lambda i, j, nf=nf: i * nf + j) for c in casts]
        cast_bytes = sum(cs[1].block_shape[0] * cs[1].block_shape[1] * (4 + 2) for cs in cast_specs)
        need = 2 * (2 * tm * d * 4 + 3 * d * tf * 2 + cast_bytes) + tm * d * 2 + FFN_TEMP_BYTES
        if need <= FFN_VMEM_BUDGET_BYTES:
            break
    outs = pl.pallas_call(
        functools.partial(_ffn_kernel, n_casts=len(casts)),
        grid=grid,
        in_specs=[
            pl.BlockSpec((tm, d), lambda i, j: (i, 0)),
            pl.BlockSpec((1, d), lambda i, j: (0, 0)),
            pl.BlockSpec((d, tf), lambda i, j: (0, j)),
            pl.BlockSpec((d, tf), lambda i, j: (0, j + nf)),
            pl.BlockSpec((tf, d), lambda i, j: (j, 0)),
        ] + [cs[0] for cs in cast_specs],
        out_specs=[pl.BlockSpec((tm, d), lambda i, j: (i, 0))] + [cs[1] for cs in cast_specs],
        out_shape=[jax.ShapeDtypeStruct((m, d), F32)] + [cs[2] for cs in cast_specs],
        scratch_shapes=[pltpu.VMEM((tm, d), BF16)],
        compiler_params=_params("arbitrary", "arbitrary"),
        name="ffn",
    )(h, norm_w.reshape(1, d), w_in, w_in, w_out, *[c.array for c in casts])
    return outs[0], list(outs[1:])


def _norm_matmul_kernel(x_ref, nw_ref, w_ref, o_ref, hn_ref):
    @pl.when(pl.program_id(1) == 0)
    def _():
        hn_ref[...] = _rms_rows(x_ref[...], nw_ref[...]).astype(BF16)

    o_ref[...] = jnp.dot(hn_ref[...], w_ref[...], preferred_element_type=F32).astype(o_ref.dtype)


def _norm_matmul(h, norm_w, w, out_dtype, name):
    m, d = h.shape
    n = w.shape[1]
    tm = _pick(m, (1408, 768, 384, 256, 128))
    tn = _pick(n, (1024, 512, 256, 128))
    return pl.pallas_call(
        _norm_matmul_kernel,
        grid=(m // tm, n // tn),
        in_specs=[
            pl.BlockSpec((tm, d), lambda i, j: (i, 0)),
            pl.BlockSpec((1, d), lambda i, j: (0, 0)),
            pl.BlockSpec((d, tn), lambda i, j: (0, j)),
        ],
        out_specs=pl.BlockSpec((tm, tn), lambda i, j: (i, j)),
        out_shape=jax.ShapeDtypeStruct((m, n), out_dtype),
        scratch_shapes=[pltpu.VMEM((tm, d), BF16)],
        compiler_params=_params("parallel", "arbitrary"),
        name=name,
    )(h, norm_w.reshape(1, d), w)


def _matmul_res_kernel(a_ref, w_ref, r_ref, o_ref):
    o_ref[...] = r_ref[...] + jnp.dot(a_ref[...], w_ref[...], preferred_element_type=F32)


def _matmul_res(a, w, res, name):
    m, k = a.shape
    n = w.shape[1]
    tm = _pick(m, (768, 384, 256, 128))
    tn = _pick(n, (1024, 512, 256, 128))
    return pl.pallas_call(
        _matmul_res_kernel,
        grid=(m // tm, n // tn),
        in_specs=[
            pl.BlockSpec((tm, k), lambda i, j: (i, 0)),
            pl.BlockSpec((k, tn), lambda i, j: (0, j)),
            pl.BlockSpec((tm, tn), lambda i, j: (i, j)),
        ],
        out_specs=pl.BlockSpec((tm, tn), lambda i, j: (i, j)),
        out_shape=jax.ShapeDtypeStruct((m, n), F32),
        compiler_params=_params("parallel", "parallel"),
        name=name,
    )(a, w, res)


def _qkv_kernel(x_ref, nw_ref, w_ref, hw_ref, cos_ref, sin_ref, o_ref, hn_ref, *, n_rope_tiles):
    j = pl.program_id(1)

    @pl.when(j == 0)
    def _():
        hn_ref[...] = _rms_rows(x_ref[...], nw_ref[...]).astype(BF16)

    tn = w_ref.shape[1]
    pw = 2 * HEAD_DIM

    @pl.when(j < n_rope_tiles)
    def _():
        acc = jnp.dot(hn_ref[...], w_ref[...], preferred_element_type=F32)
        c = jnp.concatenate([cos_ref[...]] * 2, axis=1)
        s = jnp.concatenate([sin_ref[...]] * 2, axis=1)
        src = lax.broadcasted_iota(jnp.int32, (pw, pw), 0)
        dst = lax.broadcasted_iota(jnp.int32, (pw, pw), 1)
        same_head = (src // HEAD_DIM == dst // HEAD_DIM).astype(BF16)
        partner_of = jnp.where(dst % (2 * ROPE_HALF) < ROPE_HALF, dst + ROPE_HALF, dst - ROPE_HALF)
        swap = (src == partner_of).astype(BF16)
        for p0 in range(0, tn, pw):
            sl = slice(p0, p0 + pw)
            y = acc[:, sl]
            sumsq = jnp.dot((y * y).astype(BF16), same_head, preferred_element_type=F32)
            a = y * hw_ref[:, sl]
            partner = jnp.dot(a.astype(BF16), swap, preferred_element_type=F32)
            scale = lax.rsqrt(sumsq * (1.0 / HEAD_DIM) + EPS)
            o_ref[:, sl] = ((a * c + partner * s) * scale).astype(o_ref.dtype)

    @pl.when(j >= n_rope_tiles)
    def _():
        o_ref[...] = jnp.dot(hn_ref[...], w_ref[...], preferred_element_type=F32).astype(o_ref.dtype)


def _qkv_proj(h, norm_w, w, head_w, cos, sin, seq_len, n_rope_cols):
    m, d = h.shape
    n = w.shape[1]
    tm = _pick(seq_len, (1408, 384, 128))
    tn = _pick(n_rope_cols, (512, 256))
    assert n % tn == 0 and n_rope_cols % tn == 0
    tiles_per_seq = seq_len // tm
    kern = functools.partial(_qkv_kernel, n_rope_tiles=n_rope_cols // tn)
    return pl.pallas_call(
        kern,
        grid=(m // tm, n // tn),
        in_specs=[
            pl.BlockSpec((tm, d), lambda i, j: (i, 0)),
            pl.BlockSpec((1, d), lambda i, j: (0, 0)),
            pl.BlockSpec((d, tn), lambda i, j: (0, j)),
            pl.BlockSpec((1, tn), lambda i, j: (0, j)),
            pl.BlockSpec((tm, HEAD_DIM), lambda i, j: (i % tiles_per_seq, 0)),
            pl.BlockSpec((tm, HEAD_DIM), lambda i, j: (i % tiles_per_seq, 0)),
        ],
        out_specs=pl.BlockSpec((tm, tn), lambda i, j: (i, j)),
        out_shape=jax.ShapeDtypeStruct((m, n), BF16),
        scratch_shapes=[pltpu.VMEM((tm, d), BF16)],
        compiler_params=_params("parallel", "arbitrary"),
        name="attn_qkv",
    )(h, norm_w.reshape(1, d), w, head_w, cos, sin)


def _attn_kernel(q_ref, kt_ref, v_ref, o_ref, *, n_pad, n_sub):
    kt = kt_ref[0, 0]
    v = v_ref[0]
    rs = q_ref.shape[1] // n_sub
    key_valid = lax.broadcasted_iota(jnp.int32, (1, LANES), 1) >= n_pad
    assert n_pad <= LANES
    units = [(slice(r * rs, (r + 1) * rs), slice(g * HEAD_DIM, (g + 1) * HEAD_DIM))
             for g in range(GQ) for r in range(n_sub)]

    def scores(unit):
        rows, cols = unit
        return jnp.dot(q_ref[0, rows, cols], kt, preferred_element_type=F32)

    def finish(unit, s):
        rows, cols = unit
        s = jnp.concatenate([jnp.where(key_valid, s[:, :LANES], -jnp.inf), s[:, LANES:]], axis=1)
        p = jnp.exp2(s - jnp.max(s, axis=-1, keepdims=True))
        denom = jnp.sum(p, axis=-1, keepdims=True)
        o = jnp.dot(p.astype(BF16), v, preferred_element_type=F32)
        o_ref[0, rows, cols] = (o / denom).astype(o_ref.dtype)

    s_prev = scores(units[0])
    for prev, cur in zip(units[:-1], units[1:]):
        s_cur = scores(cur)
        finish(prev, s_prev)
        s_prev = s_cur
    finish(units[-1], s_prev)


def _attention(qkv, batch, seq_len, n_heads, n_pad):
    n_kv = n_heads // GQ
    qkv3 = qkv.reshape(batch, seq_len, (n_heads + 2 * n_kv) * HEAD_DIM)
    k_cols = qkv3[:, :, n_heads * HEAD_DIM:(n_heads + n_kv) * HEAD_DIM]
    kt = k_cols.reshape(batch, seq_len, n_kv, HEAD_DIM).transpose(0, 2, 3, 1)
    tq = _pick(seq_len, (1408, 384, 128))
    n_sub = next(n for n in (11, 8, 4, 2, 1) if tq % (n * BF16_TILE_ROWS) == 0)
    qw = GQ * HEAD_DIM
    v_off = n_heads + n_kv
    return pl.pallas_call(
        functools.partial(_attn_kernel, n_pad=n_pad, n_sub=n_sub),
        grid=(batch, n_kv, seq_len // tq),
        in_specs=[
            pl.BlockSpec((1, tq, qw), lambda b, h, i: (b, i, h)),
            pl.BlockSpec((1, 1, HEAD_DIM, seq_len), lambda b, h, i: (b, h, 0, 0)),
            pl.BlockSpec((1, seq_len, HEAD_DIM), lambda b, h, i: (b, 0, v_off + h)),
        ],
        out_specs=pl.BlockSpec((1, tq, qw), lambda b, h, i: (b, i, h)),
        out_shape=jax.ShapeDtypeStruct((batch, seq_len, n_heads * HEAD_DIM), BF16),
        compiler_params=_params("parallel", "parallel", "arbitrary"),
        name="attn_core",
    )(qkv3, kt, qkv3)


def _conv_kernel(u_ref, w_ref, b_ref, o_ref, pad_ref, wb_ref, *, n_pad, halo):
    seq = u_ref.shape[1]
    tc = u_ref.shape[2]
    taps = w_ref.shape[0]
    half = (taps - 1) // 2
    nc = seq // CHUNK
    zeros = jnp.zeros((halo, tc), F32)
    pad_ref[pl.ds(0, halo), :] = zeros
    pad_ref[pl.ds(halo + seq, halo), :] = zeros

    def valid_rows(c):
        rows = c * CHUNK + lax.broadcasted_iota(jnp.int32, (CHUNK, 1), 0)
        return (rows >= n_pad).astype(F32)

    def fill(c, carry):
        r0 = pl.multiple_of(c * CHUNK, CHUNK)
        pad_ref[pl.ds(halo + r0, CHUNK), :] = u_ref[0, pl.ds(r0, CHUNK), :].astype(F32) * valid_rows(c)
        return carry

    lax.fori_loop(0, nc, fill, 0)

    win = CHUNK + 2 * halo
    k_pad = -(taps * win) % LANES
    for t in range(taps):
        wb_ref[t] = jnp.broadcast_to(w_ref[pl.ds(t, 1), :], (SUBLANES, tc))
    out_row = lax.broadcasted_iota(jnp.int32, (CHUNK, taps * win + k_pad), 0)
    col = lax.broadcasted_iota(jnp.int32, (CHUNK, taps * win + k_pad), 1)
    tap = sum((col >= m * win).astype(jnp.int32) for m in range(1, taps + 1))
    shift_sum = ((col - tap * win == out_row + (halo - half) + tap) & (tap < taps)).astype(BF16)
    bias = b_ref[...]

    def conv(c):
        r0 = pl.multiple_of(c * CHUNK, CHUNK)
        window = pad_ref[pl.ds(r0, win), :].reshape(win // SUBLANES, SUBLANES, tc)
        weighted = [(window * wb_ref[t]).reshape(win, tc).astype(BF16) for t in range(taps)]
        if k_pad:
            weighted.append(jnp.zeros((k_pad, tc), BF16))
        acc = jnp.dot(shift_sum, jnp.concatenate(weighted, axis=0), preferred_element_type=F32) + bias
        o_ref[0, pl.ds(r0, CHUNK), :] = (_silu(acc) * valid_rows(c)).astype(o_ref.dtype)

    per_body = 3 if nc % 3 == 0 else 2 if nc % 2 == 0 else 1

    def body(k, carry):
        for u in range(per_body):
            conv(k * per_body + u)
        return carry

    lax.fori_loop(0, nc // per_body, body, 0)


def _ssd_conv(zx3, conv_w, conv_b, d_inner, n_pad):
    batch, seq_len, _ = zx3.shape
    taps, ch = conv_w.shape
    tc = _pick(ch, (512, 256, 128))
    assert d_inner % tc == 0
    col0 = d_inner // tc
    halo = SUBLANES
    assert (taps - 1) // 2 < SUBLANES
    return pl.pallas_call(
        functools.partial(_conv_kernel, n_pad=n_pad, halo=halo),
        grid=(batch, ch // tc),
        in_specs=[
            pl.BlockSpec((1, seq_len, tc), lambda b, j: (b, 0, col0 + j)),
            pl.BlockSpec((taps, tc), lambda b, j: (0, j)),
            pl.BlockSpec((1, tc), lambda b, j: (0, j)),
        ],
        out_specs=pl.BlockSpec((1, seq_len, tc), lambda b, j: (b, 0, j)),
        out_shape=jax.ShapeDtypeStruct((batch, seq_len, ch), BF16),
        scratch_shapes=[pltpu.VMEM((seq_len + 2 * halo, tc), F32),
                        pltpu.VMEM((taps, SUBLANES, tc), F32)],
        compiler_params=_params("parallel", "parallel"),
        name="ssd_conv",
    )(zx3, conv_w, conv_b.reshape(1, ch))


def _softplus(x):
    return jnp.maximum(x, 0.0) + jnp.log(1.0 + jnp.exp(-jnp.abs(x)))


def _ssd_kernel(xs_ref, b_ref, c_ref, z_ref, dtr_ref, pc_ref, dskip_ref, nw_ref, g_ref, y_ref, st_ref, xw_ref,
                tabt_ref, rows_ref, *, n_pad, hpg):
    seq = xs_ref.shape[1]
    gw = xs_ref.shape[2]
    nc = seq // CHUNK
    nh = 2 * hpg
    pw = 2 * SSD_HEAD_DIM
    hi = lax.Precision.HIGHEST

    row_i = lax.broadcasted_iota(jnp.int32, (CHUNK, CHUNK), 0)
    col_i = lax.broadcasted_iota(jnp.int32, (CHUNK, CHUNK), 1)
    lower = row_i >= col_i
    upper = row_i <= col_i
    eye = row_i == col_i
    lower_f = lower.astype(F32)
    upper_f = upper.astype(F32)
    first_of_pair = lax.broadcasted_iota(jnp.int32, (1, pw), 1) < SSD_HEAD_DIM
    tab_row = lax.broadcasted_iota(jnp.int32, (CHUNK, gw), 0)
    lane_head = lax.shift_right_logical(lax.broadcasted_iota(jnp.int32, (CHUNK, gw), 1),
                                        SSD_HEAD_DIM.bit_length() - 1)
    expand = [(tab_row == nh + d * hpg + lane_head).astype(BF16) for d in range(2)]

    bias_col = pc_ref[0, :, pl.ds(0, 1)]
    aneg2_col = -jnp.exp(pc_ref[0, :, pl.ds(1, 1)]) * LOG2E
    dskip = dskip_ref[0]

    def tables_head(c, d):
        lanes = c * CHUNK + lax.broadcasted_iota(jnp.int32, (1, CHUNK), 1)
        dt_row = _softplus(dtr_ref[0, 0, c] + bias_col) * (lanes >= n_pad).astype(F32)
        a_row = dt_row * aneg2_col
        log2_dt = jnp.log2(dt_row)
        tri_f = upper_f if d == 0 else lower_f
        cs_row = jnp.dot(a_row, tri_f, precision=hi, preferred_element_type=F32)
        total = jnp.sum(a_row, axis=1, keepdims=True)
        upd_row = (total - cs_row) + log2_dt
        src_row = cs_row - log2_dt
        table_t = jnp.concatenate(
            [cs_row, upd_row, jnp.zeros((CHUNK - 2 * nh, CHUNK), F32)], axis=0).T
        return cs_row, src_row, jnp.broadcast_to(jnp.exp2(total), (nh, CHUNK)), table_t

    def tables_tail(c, d, slot, tables):
        cs_row, src_row, decay_rows, table_t = tables
        r0 = pl.multiple_of(c * CHUNK, CHUNK)
        w_exp = jnp.dot(jnp.exp2(table_t).astype(BF16), expand[d], preferred_element_type=F32)
        x = xs_ref[0, pl.ds(r0, CHUNK), :]
        xw_ref[slot, d] = (x.astype(F32) * w_exp).astype(BF16)
        tabt_ref[slot, d] = table_t
        rows_ref[slot, d, pl.ds(0, nh), :] = cs_row
        rows_ref[slot, d, pl.ds(nh, nh), :] = src_row
        rows_ref[slot, d, pl.ds(2 * nh, nh), :] = decay_rows

    def scan_head(c, d, slot):
        r0 = pl.multiple_of(c * CHUNK, CHUNK)
        x = xs_ref[0, pl.ds(r0, CHUNK), :]
        bm = b_ref[0, pl.ds(r0, CHUNK), :]
        cm = c_ref[0, pl.ds(r0, CHUNK), :]
        cb = lax.dot_general(cm, bm, (((1,), (1,)), ((), ())), preferred_element_type=F32)
        bt = bm.astype(F32).T.astype(BF16)
        upd = jnp.dot(bt, xw_ref[slot, d], preferred_element_type=F32)
        state = st_ref[d]
        z = jnp.dot(cm, state.astype(BF16), preferred_element_type=F32)
        return r0, x, cb, upd, state, z

    def scan_tail(d, slot, head):
        r0, x, cb, upd, state, z = head
        tri = lower if d == 0 else upper
        table_t = tabt_ref[slot, d]
        cs_row = rows_ref[slot, d, pl.ds(0, nh), :]
        src_row = rows_ref[slot, d, pl.ds(nh, nh), :]
        chunk_decay = rows_ref[slot, d, pl.ds(2 * nh, nh), :]
        xz = jnp.concatenate([x, z.astype(BF16)], axis=0)
        ys = []
        new_states = []
        for pair in range(hpg // 2):
            cols = slice(pair * pw, (pair + 1) * pw)
            rhs = xz[:, cols]
            h0 = d * hpg + 2 * pair
            outs = []
            for hh in (h0, h0 + 1):
                seg = jnp.where(tri, table_t[:, hh:hh + 1] - src_row[hh:hh + 1, :], -jnp.inf)
                m_diag = cb * jnp.exp2(seg)
                carry_in = jnp.where(eye, jnp.exp2(cs_row[hh:hh + 1, :]), 0.0)
                lhs = jnp.concatenate([m_diag, carry_in], axis=1).astype(BF16)
                outs.append(jnp.dot(lhs, rhs, preferred_element_type=F32))
            ys.append(jnp.where(first_of_pair, outs[0], outs[1]))
            decay = jnp.where(first_of_pair, chunk_decay[h0:h0 + 1, :], chunk_decay[h0 + 1:h0 + 2, :])
            new_states.append(decay * state[:, cols] + upd[:, cols])
        st_ref[d] = jnp.concatenate(new_states, axis=1)
        y = jnp.concatenate(ys, axis=1)
        if d == 0:
            y = y + dskip * x.astype(F32)
        y_ref[pl.ds(r0, CHUNK), :] += y

    def gate_chunk(c):
        r0 = pl.multiple_of(c * CHUNK, CHUNK)
        gated = y_ref[pl.ds(r0, CHUNK), :] * _silu(z_ref[0, pl.ds(r0, CHUNK), :].astype(F32))
        g_ref[0, pl.ds(r0, CHUNK), :] = _rms_rows(gated, nw_ref[0]).astype(g_ref.dtype)

    def step(i, slot, nxt, gate):
        heads = [scan_head(i, 0, slot), scan_head(nc - 1 - i, 1, slot)]
        if nxt is not None:
            tabs = [tables_head(nxt, 0), tables_head(nc - 1 - nxt, 1)]
        scan_tail(0, slot, heads[0])
        scan_tail(1, slot, heads[1])
        if gate:
            gate_chunk(i)
            gate_chunk(nc - 1 - i)
        if nxt is not None:
            tables_tail(nxt, 0, 1 - slot, tabs[0])
            tables_tail(nc - 1 - nxt, 1, 1 - slot, tabs[1])

    y_ref[...] = jnp.zeros(y_ref.shape, F32)
    st_ref[...] = jnp.zeros(st_ref.shape, F32)
    tables_tail(0, 0, 0, tables_head(0, 0))
    tables_tail(nc - 1, 1, 0, tables_head(nc - 1, 1))

    def two_steps(gate_first, gate_second):
        def body(k, carry):
            i = 2 * k
            step(i, 0, i + 1, gate_first)
            step(i + 1, 1, jnp.minimum(i + 2, nc - 1), gate_second)
            return carry
        return body

    first_gated = nc // 2
    plain_iters = first_gated // 2
    lax.fori_loop(0, plain_iters, two_steps(False, False), 0)
    gated_from = plain_iters
    if first_gated % 2:
        two_steps(False, True)(plain_iters, 0)
        gated_from += 1
    lax.fori_loop(gated_from, nc // 2, two_steps(True, True), 0)
    if nc % 2:
        step(nc - 1, 0, None, True)


def _ssd_core(xbc, zx3, dt_raw, dt_bias, a_log, d_skip, norm_w, batch, seq_len, d_inner, n_pad):
    groups = SSD_GROUPS
    heads = d_inner // SSD_HEAD_DIM
    hpg = heads // groups
    gw = hpg * SSD_HEAD_DIM
    nc = seq_len // CHUNK
    assert gw % (2 * SSD_HEAD_DIM) == 0 and gw % LANES == 0
    assert gw * groups == d_inner
    b_off = d_inner // SSD_STATE
    c_off = b_off + groups
    dt_rowform = dt_raw.reshape(batch, nc, CHUNK, 2, groups, hpg).transpose(0, 4, 1, 3, 5, 2)
    dt_rowform = dt_rowform.reshape(batch, groups, nc, 2 * hpg, CHUNK)
    per_head = jnp.stack([dt_bias.reshape(2, groups, hpg), a_log.reshape(2, groups, hpg)], -1)
    p_col = per_head.transpose(1, 0, 2, 3).reshape(groups, 2 * hpg, 2).astype(F32)
    d_exp = jnp.repeat(d_skip.astype(F32), SSD_HEAD_DIM).reshape(groups, 1, gw)
    return pl.pallas_call(
        functools.partial(_ssd_kernel, n_pad=n_pad, hpg=hpg),
        grid=(batch, groups),
        in_specs=[
            pl.BlockSpec((1, seq_len, gw), lambda b, g: (b, 0, g)),
            pl.BlockSpec((1, seq_len, SSD_STATE), lambda b, g: (b, 0, b_off + g)),
            pl.BlockSpec((1, seq_len, SSD_STATE), lambda b, g: (b, 0, c_off + g)),
            pl.BlockSpec((1, seq_len, gw), lambda b, g: (b, 0, g)),
            pl.BlockSpec((1, 1, nc, 2 * hpg, CHUNK), lambda b, g: (b, g, 0, 0, 0)),
            pl.BlockSpec((1, 2 * hpg, 2), lambda b, g: (g, 0, 0)),
            pl.BlockSpec((1, 1, gw), lambda b, g: (g, 0, 0)),
            pl.BlockSpec((1, 1, gw), lambda b, g: (g, 0, 0)),
        ],
        out_specs=pl.BlockSpec((1, seq_len, gw), lambda b, g: (b, 0, g)),
        out_shape=jax.ShapeDtypeStruct((batch, seq_len, d_inner), BF16),
        scratch_shapes=[
            pltpu.VMEM((seq_len, gw), F32),
            pltpu.VMEM((2, SSD_STATE, gw), F32),
            pltpu.VMEM((2, 2, CHUNK, gw), BF16),
            pltpu.VMEM((2, 2, CHUNK, CHUNK), F32),
            pltpu.VMEM((2, 2, 6 * hpg, CHUNK), F32),
        ],
        compiler_params=_params("parallel", "parallel"),
        name="ssd_core",
    )(xbc, xbc, xbc, zx3, dt_rowform, p_col, d_exp, norm_w.astype(F32).reshape(groups, 1, gw))


def _ssd_mixer(h, batch, seq_len, n_pad, mix_norm, w_zx, w_dt, out_proj, conv_w, conv_b, dt_bias, a_log,
               d_skip, norm_w):
    d_inner = out_proj.shape[0]
    n_zx = w_zx.shape[1]
    zx = _norm_matmul(h, mix_norm, w_zx, BF16, "ssd_in_zx")
    dt_raw = _norm_matmul(h, mix_norm, w_dt, F32, "ssd_in_dt")
    zx3 = zx.reshape(batch, seq_len, n_zx)
    xbc = _ssd_conv(zx3, conv_w, conv_b, d_inner, n_pad)
    g = _ssd_core(xbc, zx3, dt_raw, dt_bias, a_log, d_skip, norm_w, batch, seq_len, d_inner, n_pad)
    return _matmul_res(g.reshape(batch * seq_len, d_inner), out_proj, h, "ssd_out")


def _rope_tables(seq_len, n_pad, n_meta):
    n_tok = seq_len - n_pad - n_meta
    rows_n = n_tok // GRID_W
    row = jnp.concatenate([jnp.zeros((n_pad,), jnp.int32), jnp.full((n_meta,), -1, jnp.int32),
                           jnp.repeat(jnp.arange(rows_n, dtype=jnp.int32), GRID_W)])
    col = jnp.concatenate([jnp.zeros((n_pad,), jnp.int32), jnp.arange(n_meta, dtype=jnp.int32),
                           jnp.tile(jnp.arange(GRID_W, dtype=jnp.int32), rows_n)])
    inv_freq = ROPE_THETA ** (-jnp.arange(0, 2 * ROPE_HALF, 2, dtype=F32) / (2 * ROPE_HALF))
    ang_r = row.astype(F32)[:, None] * inv_freq
    ang_c = col.astype(F32)[:, None] * inv_freq
    cos = jnp.concatenate([jnp.cos(ang_r), jnp.cos(ang_r), jnp.cos(ang_c), jnp.cos(ang_c)], -1)
    sin = jnp.concatenate([-jnp.sin(ang_r), jnp.sin(ang_r), -jnp.sin(ang_c), jnp.sin(ang_c)], -1)
    return cos, sin


def _attention_mixer(h, batch, seq_len, n_pad, cos, sin, mix_norm, w_qkv, w_o, q_norm, k_norm):
    n_heads = w_o.shape[0] // HEAD_DIM
    n_kv = n_heads // GQ
    head_w = jnp.concatenate([jnp.tile(q_norm.astype(F32) * (HEAD_DIM ** -0.5 * LOG2E), n_heads),
                              jnp.tile(k_norm.astype(F32), n_kv),
                              jnp.ones((n_kv * HEAD_DIM,), F32)]).reshape(1, -1)
    qkv = _qkv_proj(h, mix_norm, w_qkv, head_w, cos, sin, seq_len, (n_heads + n_kv) * HEAD_DIM)
    o = _attention(qkv, batch, seq_len, n_heads, n_pad)
    return _matmul_res(o.reshape(batch * seq_len, n_heads * HEAD_DIM), w_o, h, "attn_out")


def kernel(x, meta_tokens, ffn_norm, ffn_w_in, ffn_w_out, mix_norm, ssd_in_proj, ssd_conv_w, ssd_conv_b,
           ssd_dt_bias, ssd_A_log, ssd_D, ssd_norm, ssd_out_proj, attn_w_qkv, attn_q_norm, attn_k_norm,
           attn_w_o):
    batch, n_tok, d = x.shape
    n_meta = meta_tokens.shape[0]
    n_pad = CHUNK - n_meta
    seq_len = n_pad + n_meta + n_tok
    depth = ffn_norm.shape[0]
    cos, sin = _rope_tables(seq_len, n_pad, n_meta)
    h = jnp.concatenate([jnp.zeros((batch, n_pad, d), x.dtype),
                         jnp.broadcast_to(meta_tokens.astype(x.dtype)[None], (batch, n_meta, d)),
                         x], axis=1).reshape(batch * seq_len, d)
    n_dt = 2 * (ssd_out_proj.shape[1] // SSD_HEAD_DIM)
    n_zx = ssd_in_proj.shape[2] - n_dt
    w_in, w_out = ffn_w_in[0, 0].astype(BF16), ffn_w_out[0, 0].astype(BF16)
    for i in range(depth):
        j = i // 2
        if i % 2 == 0:
            mixer_casts = [_Cast(ssd_in_proj, (j,), 0, n_zx), _Cast(ssd_in_proj, (j,), n_zx, n_dt),
                           _Cast(ssd_out_proj, (j,))]
        else:
            mixer_casts = [_Cast(attn_w_qkv, (j,)), _Cast(attn_w_o, (j,))]
        h, cast = _ffn(h, ffn_norm[i, 0], w_in, w_out,
                       mixer_casts + [_Cast(ffn_w_in, (i, 1)), _Cast(ffn_w_out, (i, 1))])
        *mixer_w, w_in, w_out = cast
        if i % 2 == 0:
            h = _ssd_mixer(h, batch, seq_len, n_pad, mix_norm[i], *mixer_w, ssd_conv_w[j], ssd_conv_b[j],
                           ssd_dt_bias[j], ssd_A_log[j], ssd_D[j], ssd_norm[j])
        else:
            h = _attention_mixer(h, batch, seq_len, n_pad, cos, sin, mix_norm[i], *mixer_w,
                                 attn_q_norm[j], attn_k_norm[j])
        last = i + 1 == depth
        h, cast = _ffn(h, ffn_norm[i, 1], w_in, w_out,
                       [] if last else [_Cast(ffn_w_in, (i + 1, 0)), _Cast(ffn_w_out, (i + 1, 0))])
        if not last:
            w_in, w_out = cast
    return h.reshape(batch, seq_len, d)[:, n_pad + n_meta:, :]
```

```python
import functools

import jax
import jax.numpy as jnp
from jax import lax
from jax.experimental import pallas as pl
from jax.experimental.pallas import tpu as pltpu

F32 = jnp.float32
BF16 = jnp.bfloat16

GRID_W = 64
CHUNK = 128
EPS = 1e-6
SSD_HEAD_DIM = 64
SSD_GROUPS = 8
SSD_STATE = 128
HEAD_DIM = 128
GQ = 2
ROPE_THETA = 10000.0
ROPE_HALF = HEAD_DIM // 4
LOG2E = 1.4426950408889634

V7X_VMEM_LIMIT_BYTES = 58 * 1024 * 1024
FFN_VMEM_BUDGET_BYTES = 56 * 1024 * 1024
FFN_TEMP_BYTES = 3 * 1024 * 1024
LANES = 128
SUBLANES = 8
BF16_TILE_ROWS = 16


def _params(*sem):
    return pltpu.CompilerParams(dimension_semantics=sem, vmem_limit_bytes=V7X_VMEM_LIMIT_BYTES)


def _pick(n, candidates):
    for c in candidates:
        if n % c == 0:
            return c
    return n


def _rms_rows(x, w):
    ms = jnp.mean(x * x, axis=-1, keepdims=True)
    return x * lax.rsqrt(ms + EPS) * w


def _silu(x):
    half = 0.5 * x
    return half + half * jnp.tanh(half)


def _ffn_kernel(x_ref, nw_ref, wg_ref, wu_ref, wo_ref, *refs, n_casts):
    cast_in, o_ref, cast_out, hn_ref = refs[:n_casts], refs[n_casts], refs[n_casts + 1:-1], refs[-1]
    j = pl.program_id(1)

    @pl.when(j == 0)
    def _():
        x = x_ref[...]
        hn_ref[...] = _rms_rows(x, nw_ref[...]).astype(BF16)
        o_ref[...] = x

    hn = hn_ref[...]
    g = jnp.dot(hn, wg_ref[...], preferred_element_type=F32)
    u = jnp.dot(hn, wu_ref[...], preferred_element_type=F32)
    a = (0.5 * _silu(g) * u).astype(BF16)
    o_ref[...] += jnp.dot(a, wo_ref[...], preferred_element_type=F32)
    for src, dst in zip(cast_in, cast_out):
        dst[...] = src[...].astype(BF16)


class _Cast:
    def __init__(self, array, lead, col0=0, cols=None):
        self.array, self.lead, self.col0 = array, tuple(lead), col0
        self.rows = array.shape[-2]
        self.cols = array.shape[-1] - col0 if cols is None else cols
        assert col0 % self.cols == 0

    def specs(self, n_steps, step_of):
        nb = 1
        while 2 * nb <= n_steps and self.rows % (2 * nb) == 0 and (self.rows // (2 * nb)) % BF16_TILE_ROWS == 0:
            nb *= 2
        rb = self.rows // nb
        lead, cb = self.lead, self.col0 // self.cols

        def block(*grid_idx):
            return jnp.minimum(step_of(*grid_idx), nb - 1)

        in_spec = pl.BlockSpec((None,) * len(lead) + (rb, self.cols), lambda *g: lead + (block(*g), cb))
        out_spec = pl.BlockSpec((rb, self.cols), lambda *g: (block(*g), 0))
        return in_spec, out_spec, jax.ShapeDtypeStruct((self.rows, self.cols), BF16)


def _ffn(h, norm_w, w_in, w_out, casts=()):
    m, d = h.shape
    f = w_out.shape[0]
    tm = _pick(m, (768, 384, 256, 128))
    for tf in (768, 512, 256, 128):
        if f % tf:
            continue
        nf = f // tf
        grid = (m // tm, nf)
        cast_specs = [c.specs(grid[0] * grid[1], lambda i, j, nf=nf: i * nf + j) for c in casts]
        cast_bytes = sum(cs[1].block_shape[0] * cs[1].block_shape[1] * (4 + 2) for cs in cast_specs)
        need = 2 * (2 * tm * d * 4 + 3 * d * tf * 2 + cast_bytes) + tm * d * 2 + FFN_TEMP_BYTES
        if need <= FFN_VMEM_BUDGET_BYTES:
            break
    outs = pl.pallas_call(
        functools.partial(_ffn_kernel, n_casts=len(casts)),
        grid=grid,
        in_specs=[
            pl.BlockSpec((tm, d), lambda i, j: (i, 0)),
            pl.BlockSpec((1, d), lambda i, j: (0, 0)),
            pl.BlockSpec((d, tf), lambda i, j: (0, j)),
            pl.BlockSpec((d, tf), lambda i, j: (0, j + nf)),
            pl.BlockSpec((tf, d), lambda i, j: (j, 0)),
        ] + [cs[0] for cs in cast_specs],
        out_specs=[pl.BlockSpec((tm, d), lambda i, j: (i, 0))] + [cs[1] for cs in cast_specs],
        out_shape=[jax.ShapeDtypeStruct((m, d), F32)] + [cs[2] for cs in cast_specs],
        scratch_shapes=[pltpu.VMEM((tm, d), BF16)],
        compiler_params=_params("arbitrary", "arbitrary"),
        name="ffn",
    )(h, norm_w.reshape(1, d), w_in, w_in, w_out, *[c.array for c in casts])
    return outs[0], list(outs[1:])


def _norm_matmul_kernel(x_ref, nw_ref, w_ref, o_ref, hn_ref):
    @pl.when(pl.program_id(1) == 0)
    def _():
        hn_ref[...] = _rms_rows(x_ref[...], nw_ref[...]).astype(BF16)

    o_ref[...] = jnp.dot(hn_ref[...], w_ref[...], preferred_element_type=F32).astype(o_ref.dtype)


def _norm_matmul(h, norm_w, w, out_dtype, name):
    m, d = h.shape
    n = w.shape[1]
    tm = _pick(m, (1408, 768, 384, 256, 128))
    tn = _pick(n, (1024, 512, 256, 128))
    return pl.pallas_call(
        _norm_matmul_kernel,
        grid=(m // tm, n // tn),
        in_specs=[
            pl.BlockSpec((tm, d), lambda i, j: (i, 0)),
            pl.BlockSpec((1, d), lambda i, j: (0, 0)),
            pl.BlockSpec((d, tn), lambda i, j: (0, j)),
        ],
        out_specs=pl.BlockSpec((tm, tn), lambda i, j: (i, j)),
        out_shape=jax.ShapeDtypeStruct((m, n), out_dtype),
        scratch_shapes=[pltpu.VMEM((tm, d), BF16)],
        compiler_params=_params("parallel", "arbitrary"),
        name=name,
    )(h, norm_w.reshape(1, d), w)


def _matmul_res_kernel(a_ref, w_ref, r_ref, o_ref):
    o_ref[...] = r_ref[...] + jnp.dot(a_ref[...], w_ref[...], preferred_element_type=F32)


def _matmul_res(a, w, res, name):
    m, k = a.shape
    n = w.shape[1]
    tm = _pick(m, (768, 384, 256, 128))
    tn = _pick(n, (1024, 512, 256, 128))
    return pl.pallas_call(
        _matmul_res_kernel,
        grid=(m // tm, n // tn),
        in_specs=[
            pl.BlockSpec((tm, k), lambda i, j: (i, 0)),
            pl.BlockSpec((k, tn), lambda i, j: (0, j)),
            pl.BlockSpec((tm, tn), lambda i, j: (i, j)),
        ],
        out_specs=pl.BlockSpec((tm, tn), lambda i, j: (i, j)),
        out_shape=jax.ShapeDtypeStruct((m, n), F32),
        compiler_params=_params("parallel", "parallel"),
        name=name,
    )(a, w, res)


def _qkv_kernel(x_ref, nw_ref, w_ref, hw_ref, cos_ref, sin_ref, o_ref, hn_ref, *, n_rope_tiles):
    j = pl.program_id(1)

    @pl.when(j == 0)
    def _():
        hn_ref[...] = _rms_rows(x_ref[...], nw_ref[...]).astype(BF16)

    tn = w_ref.shape[1]
    pw = 2 * HEAD_DIM

    @pl.when(j < n_rope_tiles)
    def _():
        acc = jnp.dot(hn_ref[...], w_ref[...], preferred_element_type=F32)
        c = jnp.concatenate([cos_ref[...]] * 2, axis=1)
        s = jnp.concatenate([sin_ref[...]] * 2, axis=1)
        src = lax.broadcasted_iota(jnp.int32, (pw, pw), 0)
        dst = lax.broadcasted_iota(jnp.int32, (pw, pw), 1)
        same_head = (src // HEAD_DIM == dst // HEAD_DIM).astype(BF16)
        partner_of = jnp.where(dst % (2 * ROPE_HALF) < ROPE_HALF, dst + ROPE_HALF, dst - ROPE_HALF)
        swap = (src == partner_of).astype(BF16)
        for p0 in range(0, tn, pw):
            sl = slice(p0, p0 + pw)
            y = acc[:, sl]
            sumsq = jnp.dot((y * y).astype(BF16), same_head, preferred_element_type=F32)
            a = y * hw_ref[:, sl]
            partner = jnp.dot(a.astype(BF16), swap, preferred_element_type=F32)
            scale = lax.rsqrt(sumsq * (1.0 / HEAD_DIM) + EPS)
            o_ref[:, sl] = ((a * c + partner * s) * scale).astype(o_ref.dtype)

    @pl.when(j >= n_rope_tiles)
    def _():
        o_ref[...] = jnp.dot(hn_ref[...], w_ref[...], preferred_element_type=F32).astype(o_ref.dtype)


def _qkv_proj(h, norm_w, w, head_w, cos, sin, seq_len, n_rope_cols):
    m, d = h.shape
    n = w.shape[1]
    tm = _pick(seq_len, (1408, 384, 128))
    tn = _pick(n_rope_cols, (512, 256))
    assert n % tn == 0 and n_rope_cols % tn == 0
    tiles_per_seq = seq_len // tm
    kern = functools.partial(_qkv_kernel, n_rope_tiles=n_rope_cols // tn)
    return pl.pallas_call(
        kern,
        grid=(m // tm, n // tn),
        in_specs=[
            pl.BlockSpec((tm, d), lambda i, j: (i, 0)),
            pl.BlockSpec((1, d), lambda i, j: (0, 0)),
            pl.BlockSpec((d, tn), lambda i, j: (0, j)),
            pl.BlockSpec((1, tn), lambda i, j: (0, j)),
            pl.BlockSpec((tm, HEAD_DIM), lambda i, j: (i % tiles_per_seq, 0)),
            pl.BlockSpec((tm, HEAD_DIM), lambda i, j: (i % tiles_per_seq, 0)),
        ],
        out_specs=pl.BlockSpec((tm, tn), lambda i, j: (i, j)),
        out_shape=jax.ShapeDtypeStruct((m, n), BF16),
        scratch_shapes=[pltpu.VMEM((tm, d), BF16)],
        compiler_params=_params("parallel", "arbitrary"),
        name="attn_qkv",
    )(h, norm_w.reshape(1, d), w, head_w, cos, sin)


def _attn_kernel(q_ref, kt_ref, v_ref, o_ref, *, n_pad, n_sub):
    kt = kt_ref[0, 0]
    v = v_ref[0]
    rs = q_ref.shape[1] // n_sub
    key_valid = lax.broadcasted_iota(jnp.int32, (1, LANES), 1) >= n_pad
    assert n_pad <= LANES
    units = [(slice(r * rs, (r + 1) * rs), slice(g * HEAD_DIM, (g + 1) * HEAD_DIM))
             for g in range(GQ) for r in range(n_sub)]

    def scores(unit):
        rows, cols = unit
        return jnp.dot(q_ref[0, rows, cols], kt, preferred_element_type=F32)

    def finish(unit, s):
        rows, cols = unit
        s = jnp.concatenate([jnp.where(key_valid, s[:, :LANES], -jnp.inf), s[:, LANES:]], axis=1)
        p = jnp.exp2(s - jnp.max(s, axis=-1, keepdims=True))
        denom = jnp.sum(p, axis=-1, keepdims=True)
        o = jnp.dot(p.astype(BF16), v, preferred_element_type=F32)
        o_ref[0, rows, cols] = (o / denom).astype(o_ref.dtype)

    s_prev = scores(units[0])
    for prev, cur in zip(units[:-1], units[1:]):
        s_cur = scores(cur)
        finish(prev, s_prev)
        s_prev = s_cur
    finish(units[-1], s_prev)


def _attention(qkv, batch, seq_len, n_heads, n_pad):
    n_kv = n_heads // GQ
    qkv3 = qkv.reshape(batch, seq_len, (n_heads + 2 * n_kv) * HEAD_DIM)
    k_cols = qkv3[:, :, n_heads * HEAD_DIM:(n_heads + n_kv) * HEAD_DIM]
    kt = k_cols.reshape(batch, seq_len, n_kv, HEAD_DIM).transpose(0, 2, 3, 1)
    tq = _pick(seq_len, (1408, 384, 128))
    n_sub = next(n for n in (11, 8, 4, 2, 1) if tq % (n * BF16_TILE_ROWS) == 0)
    qw = GQ * HEAD_DIM
    v_off = n_heads + n_kv
    return pl.pallas_call(
        functools.partial(_attn_kernel, n_pad=n_pad, n_sub=n_sub),
        grid=(batch, n_kv, seq_len // tq),
        in_specs=[
            pl.BlockSpec((1, tq, qw), lambda b, h, i: (b, i, h)),
            pl.BlockSpec((1, 1, HEAD_DIM, seq_len), lambda b, h, i: (b, h, 0, 0)),
            pl.BlockSpec((1, seq_len, HEAD_DIM), lambda b, h, i: (b, 0, v_off + h)),
        ],
        out_specs=pl.BlockSpec((1, tq, qw), lambda b, h, i: (b, i, h)),
        out_shape=jax.ShapeDtypeStruct((batch, seq_len, n_heads * HEAD_DIM), BF16),
        compiler_params=_params("parallel", "parallel", "arbitrary"),
        name="attn_core",
    )(qkv3, kt, qkv3)


def _conv_kernel(u_ref, w_ref, b_ref, o_ref, pad_ref, wb_ref, *, n_pad, halo):
    seq = u_ref.shape[1]
    tc = u_ref.shape[2]
    taps = w_ref.shape[0]
    half = (taps - 1) // 2
    nc = seq // CHUNK
    zeros = jnp.zeros((halo, tc), F32)
    pad_ref[pl.ds(0, halo), :] = zeros
    pad_ref[pl.ds(halo + seq, halo), :] = zeros

    def valid_rows(c):
        rows = c * CHUNK + lax.broadcasted_iota(jnp.int32, (CHUNK, 1), 0)
        return (rows >= n_pad).astype(F32)

    def fill(c, carry):
        r0 = pl.multiple_of(c * CHUNK, CHUNK)
        pad_ref[pl.ds(halo + r0, CHUNK), :] = u_ref[0, pl.ds(r0, CHUNK), :].astype(F32) * valid_rows(c)
        return carry

    lax.fori_loop(0, nc, fill, 0)

    win = CHUNK + 2 * halo
    k_pad = -(taps * win) % LANES
    for t in range(taps):
        wb_ref[t] = jnp.broadcast_to(w_ref[pl.ds(t, 1), :], (SUBLANES, tc))
    out_row = lax.broadcasted_iota(jnp.int32, (CHUNK, taps * win + k_pad), 0)
    col = lax.broadcasted_iota(jnp.int32, (CHUNK, taps * win + k_pad), 1)
    tap = sum((col >= m * win).astype(jnp.int32) for m in range(1, taps + 1))
    shift_sum = ((col - tap * win == out_row + (halo - half) + tap) & (tap < taps)).astype(BF16)
    bias = b_ref[...]

    def conv(c):
        r0 = pl.multiple_of(c * CHUNK, CHUNK)
        window = pad_ref[pl.ds(r0, win), :].reshape(win // SUBLANES, SUBLANES, tc)
        weighted = [(window * wb_ref[t]).reshape(win, tc).astype(BF16) for t in range(taps)]
        if k_pad:
            weighted.append(jnp.zeros((k_pad, tc), BF16))
        acc = jnp.dot(shift_sum, jnp.concatenate(weighted, axis=0), preferred_element_type=F32) + bias
        o_ref[0, pl.ds(r0, CHUNK), :] = (_silu(acc) * valid_rows(c)).astype(o_ref.dtype)

    per_body = 3 if nc % 3 == 0 else 2 if nc % 2 == 0 else 1

    def body(k, carry):
        for u in range(per_body):
            conv(k * per_body + u)
        return carry

    lax.fori_loop(0, nc // per_body, body, 0)


def _ssd_conv(zx3, conv_w, conv_b, d_inner, n_pad):
    batch, seq_len, _ = zx3.shape
    taps, ch = conv_w.shape
    tc = _pick(ch, (512, 256, 128))
    assert d_inner % tc == 0
    col0 = d_inner // tc
    halo = SUBLANES
    assert (taps - 1) // 2 < SUBLANES
    return pl.pallas_call(
        functools.partial(_conv_kernel, n_pad=n_pad, halo=halo),
        grid=(batch, ch // tc),
        in_specs=[
            pl.BlockSpec((1, seq_len, tc), lambda b, j: (b, 0, col0 + j)),
            pl.BlockSpec((taps, tc), lambda b, j: (0, j)),
            pl.BlockSpec((1, tc), lambda b, j: (0, j)),
        ],
        out_specs=pl.BlockSpec((1, seq_len, tc), lambda b, j: (b, 0, j)),
        out_shape=jax.ShapeDtypeStruct((batch, seq_len, ch), BF16),
        scratch_shapes=[pltpu.VMEM((seq_len + 2 * halo, tc), F32),
                        pltpu.VMEM((taps, SUBLANES, tc), F32)],
        compiler_params=_params("parallel", "parallel"),
        name="ssd_conv",
    )(zx3, conv_w, conv_b.reshape(1, ch))


def _softplus(x):
    return jnp.maximum(x, 0.0) + jnp.log(1.0 + jnp.exp(-jnp.abs(x)))


def _ssd_kernel(xs_ref, b_ref, c_ref, z_ref, dtr_ref, pc_ref, dskip_ref, nw_ref, g_ref, y_ref, st_ref, xw_ref,
                tabt_ref, rows_ref, *, n_pad, hpg):
    seq = xs_ref.shape[1]
    gw = xs_ref.shape[2]
    nc = seq // CHUNK
    nh = 2 * hpg
    pw = 2 * SSD_HEAD_DIM

    row_i = lax.broadcasted_iota(jnp.int32, (CHUNK, CHUNK), 0)
    col_i = lax.broadcasted_iota(jnp.int32, (CHUNK, CHUNK), 1)
    lower = row_i >= col_i
    upper = row_i <= col_i
    eye = row_i == col_i
    lower3 = jnp.concatenate([lower.astype(BF16)] * 3, axis=0)
    upper3 = jnp.concatenate([upper.astype(BF16)] * 3, axis=0)
    first_of_pair = lax.broadcasted_iota(jnp.int32, (1, pw), 1) < SSD_HEAD_DIM
    tab_row = lax.broadcasted_iota(jnp.int32, (CHUNK, gw), 0)
    lane_head = lax.shift_right_logical(lax.broadcasted_iota(jnp.int32, (CHUNK, gw), 1),
                                        SSD_HEAD_DIM.bit_length() - 1)
    expand = [(tab_row == nh + d * hpg + lane_head).astype(BF16) for d in range(2)]

    bias_col = pc_ref[0, :, pl.ds(0, 1)]
    aneg2_col = -jnp.exp(pc_ref[0, :, pl.ds(1, 1)]) * LOG2E
    dskip = dskip_ref[0]

    def tables_head(c, d):
        lanes = c * CHUNK + lax.broadcasted_iota(jnp.int32, (1, CHUNK), 1)
        dt_row = _softplus(dtr_ref[0, 0, c] + bias_col) * (lanes >= n_pad).astype(F32)
        a_row = dt_row * aneg2_col
        log2_dt = jnp.log2(dt_row)
        tri3 = upper3 if d == 0 else lower3
        p1 = a_row.astype(BF16)
        r1 = a_row - p1.astype(F32)
        p2 = r1.astype(BF16)
        p3 = (r1 - p2.astype(F32)).astype(BF16)
        cs_row = jnp.dot(jnp.concatenate([p1, p2, p3], axis=1), tri3, preferred_element_type=F32)
        total = jnp.sum(a_row, axis=1, keepdims=True)
        upd_row = (total - cs_row) + log2_dt
        src_row = cs_row - log2_dt
        table_t = jnp.concatenate(
            [cs_row, upd_row, jnp.zeros((CHUNK - 2 * nh, CHUNK), F32)], axis=0).T
        return cs_row, src_row, jnp.broadcast_to(jnp.exp2(total), (nh, CHUNK)), table_t

    def tables_tail(c, d, slot, tables):
        cs_row, src_row, decay_rows, table_t = tables
        r0 = pl.multiple_of(c * CHUNK, CHUNK)
        w_exp = jnp.dot(jnp.exp2(table_t).astype(BF16), expand[d], preferred_element_type=F32)
        x = xs_ref[0, pl.ds(r0, CHUNK), :]
        xw_ref[slot, d] = (x.astype(F32) * w_exp).astype(BF16)
        tabt_ref[slot, d] = table_t
        rows_ref[slot, d, pl.ds(0, nh), :] = cs_row
        rows_ref[slot, d, pl.ds(nh, nh), :] = src_row
        rows_ref[slot, d, pl.ds(2 * nh, nh), :] = decay_rows

    def scan_head(c, d, slot):
        r0 = pl.multiple_of(c * CHUNK, CHUNK)
        x = xs_ref[0, pl.ds(r0, CHUNK), :]
        bm = b_ref[0, pl.ds(r0, CHUNK), :]
        cm = c_ref[0, pl.ds(r0, CHUNK), :]
        cb = lax.dot_general(cm, bm, (((1,), (1,)), ((), ())), preferred_element_type=F32)
        bt = bm.astype(F32).T.astype(BF16)
        upd = jnp.dot(bt, xw_ref[slot, d], preferred_element_type=F32)
        state = st_ref[d]
        z = jnp.dot(cm, state.astype(BF16), preferred_element_type=F32)
        return r0, x, cb, upd, state, z

    def scan_tail(d, slot, head, accumulate):
        r0, x, cb, upd, state, z = head
        tri = lower if d == 0 else upper
        table_t = tabt_ref[slot, d]
        cs_row = rows_ref[slot, d, pl.ds(0, nh), :]
        src_row = rows_ref[slot, d, pl.ds(nh, nh), :]
        chunk_decay = rows_ref[slot, d, pl.ds(2 * nh, nh), :]
        xz = jnp.concatenate([x, z.astype(BF16)], axis=0)
        ys = []
        new_states = []
        for pair in range(hpg // 2):
            cols = slice(pair * pw, (pair + 1) * pw)
            rhs = xz[:, cols]
            h0 = d * hpg + 2 * pair
            outs = []
            for hh in (h0, h0 + 1):
                seg = jnp.where(tri, table_t[:, hh:hh + 1] - src_row[hh:hh + 1, :], -jnp.inf)
                m_diag = cb * jnp.exp2(seg)
                carry_in = jnp.where(eye, jnp.exp2(cs_row[hh:hh + 1, :]), 0.0)
                lhs = jnp.concatenate([m_diag, carry_in], axis=1).astype(BF16)
                outs.append(jnp.dot(lhs, rhs, preferred_element_type=F32))
            ys.append(jnp.where(first_of_pair, outs[0], outs[1]))
            decay = jnp.where(first_of_pair, chunk_decay[h0:h0 + 1, :], chunk_decay[h0 + 1:h0 + 2, :])
            new_states.append(decay * state[:, cols] + upd[:, cols])
        st_ref[d] = jnp.concatenate(new_states, axis=1)
        y = jnp.concatenate(ys, axis=1)
        if d == 0:
            y = y + dskip * x.astype(F32)
        if accumulate:
            y_ref[pl.ds(r0, CHUNK), :] += y
        else:
            y_ref[pl.ds(r0, CHUNK), :] = y

    def gate_chunk(c):
        r0 = pl.multiple_of(c * CHUNK, CHUNK)
        gated = y_ref[pl.ds(r0, CHUNK), :] * _silu(z_ref[0, pl.ds(r0, CHUNK), :].astype(F32))
        g_ref[0, pl.ds(r0, CHUNK), :] = _rms_rows(gated, nw_ref[0]).astype(g_ref.dtype)

    def step(i, slot, nxt, gate):
        heads = [scan_head(i, 0, slot), scan_head(nc - 1 - i, 1, slot)]
        if nxt is not None:
            tabs = [tables_head(nxt, 0), tables_head(nc - 1 - nxt, 1)]
        scan_tail(0, slot, heads[0], gate)
        scan_tail(1, slot, heads[1], gate)
        if gate:
            gate_chunk(i)
            gate_chunk(nc - 1 - i)
        if nxt is not None:
            tables_tail(nxt, 0, 1 - slot, tabs[0])
            tables_tail(nc - 1 - nxt, 1, 1 - slot, tabs[1])

    if nc % 2:
        y_ref[pl.ds((nc // 2) * CHUNK, CHUNK), :] = jnp.zeros((CHUNK, gw), F32)
    st_ref[...] = jnp.zeros(st_ref.shape, F32)
    tables_tail(0, 0, 0, tables_head(0, 0))
    tables_tail(nc - 1, 1, 0, tables_head(nc - 1, 1))

    def two_steps(gate_first, gate_second):
        def body(k, carry):
            i = 2 * k
            step(i, 0, i + 1, gate_first)
            step(i + 1, 1, jnp.minimum(i + 2, nc - 1), gate_second)
            return carry
        return body

    first_gated = nc // 2
    plain_iters = first_gated // 2
    lax.fori_loop(0, plain_iters, two_steps(False, False), 0)
    gated_from = plain_iters
    if first_gated % 2:
        two_steps(False, True)(plain_iters, 0)
        gated_from += 1
    lax.fori_loop(gated_from, nc // 2, two_steps(True, True), 0)
    if nc % 2:
        step(nc - 1, 0, None, True)


def _ssd_core(xbc, zx3, dt_raw, dt_bias, a_log, d_skip, norm_w, batch, seq_len, d_inner, n_pad):
    groups = SSD_GROUPS
    heads = d_inner // SSD_HEAD_DIM
    hpg = heads // groups
    gw = hpg * SSD_HEAD_DIM
    nc = seq_len // CHUNK
    assert gw % (2 * SSD_HEAD_DIM) == 0 and gw % LANES == 0
    assert gw * groups == d_inner
    b_off = d_inner // SSD_STATE
    c_off = b_off + groups
    dt_rowform = dt_raw.reshape(batch, nc, CHUNK, 2, groups, hpg).transpose(0, 4, 1, 3, 5, 2)
    dt_rowform = dt_rowform.reshape(batch, groups, nc, 2 * hpg, CHUNK)
    per_head = jnp.stack([dt_bias.reshape(2, groups, hpg), a_log.reshape(2, groups, hpg)], -1)
    p_col = per_head.transpose(1, 0, 2, 3).reshape(groups, 2 * hpg, 2).astype(F32)
    d_exp = jnp.repeat(d_skip.astype(F32), SSD_HEAD_DIM).reshape(groups, 1, gw)
    return pl.pallas_call(
        functools.partial(_ssd_kernel, n_pad=n_pad, hpg=hpg),
        grid=(batch, groups),
        in_specs=[
            pl.BlockSpec((1, seq_len, gw), lambda b, g: (b, 0, g)),
            pl.BlockSpec((1, seq_len, SSD_STATE), lambda b, g: (b, 0, b_off + g)),
            pl.BlockSpec((1, seq_len, SSD_STATE), lambda b, g: (b, 0, c_off + g)),
            pl.BlockSpec((1, seq_len, gw), lambda b, g: (b, 0, g)),
            pl.BlockSpec((1, 1, nc, 2 * hpg, CHUNK), lambda b, g: (b, g, 0, 0, 0)),
            pl.BlockSpec((1, 2 * hpg, 2), lambda b, g: (g, 0, 0)),
            pl.BlockSpec((1, 1, gw), lambda b, g: (g, 0, 0)),
            pl.BlockSpec((1, 1, gw), lambda b, g: (g, 0, 0)),
        ],
        out_specs=pl.BlockSpec((1, seq_len, gw), lambda b, g: (b, 0, g)),
        out_shape=jax.ShapeDtypeStruct((batch, seq_len, d_inner), BF16),
        scratch_shapes=[
            pltpu.VMEM((seq_len, gw), F32),
            pltpu.VMEM((2, SSD_STATE, gw), F32),
            pltpu.VMEM((2, 2, CHUNK, gw), BF16),
            pltpu.VMEM((2, 2, CHUNK, CHUNK), F32),
            pltpu.VMEM((2, 2, 6 * hpg, CHUNK), F32),
        ],
        compiler_params=_params("parallel", "parallel"),
        name="ssd_core",
    )(xbc, xbc, xbc, zx3, dt_rowform, p_col, d_exp, norm_w.astype(F32).reshape(groups, 1, gw))


def _ssd_mixer(h, batch, seq_len, n_pad, mix_norm, w_zx, w_dt, out_proj, conv_w, conv_b, dt_bias, a_log,
               d_skip, norm_w):
    d_inner = out_proj.shape[0]
    n_zx = w_zx.shape[1]
    zx = _norm_matmul(h, mix_norm, w_zx, BF16, "ssd_in_zx")
    dt_raw = _norm_matmul(h, mix_norm, w_dt, F32, "ssd_in_dt")
    zx3 = zx.reshape(batch, seq_len, n_zx)
    xbc = _ssd_conv(zx3, conv_w, conv_b, d_inner, n_pad)
    g = _ssd_core(xbc, zx3, dt_raw, dt_bias, a_log, d_skip, norm_w, batch, seq_len, d_inner, n_pad)
    return _matmul_res(g.reshape(batch * seq_len, d_inner), out_proj, h, "ssd_out")


def _rope_tables(seq_len, n_pad, n_meta):
    n_tok = seq_len - n_pad - n_meta
    rows_n = n_tok // GRID_W
    row = jnp.concatenate([jnp.zeros((n_pad,), jnp.int32), jnp.full((n_meta,), -1, jnp.int32),
                           jnp.repeat(jnp.arange(rows_n, dtype=jnp.int32), GRID_W)])
    col = jnp.concatenate([jnp.zeros((n_pad,), jnp.int32), jnp.arange(n_meta, dtype=jnp.int32),
                           jnp.tile(jnp.arange(GRID_W, dtype=jnp.int32), rows_n)])
    inv_freq = ROPE_THETA ** (-jnp.arange(0, 2 * ROPE_HALF, 2, dtype=F32) / (2 * ROPE_HALF))
    ang_r = row.astype(F32)[:, None] * inv_freq
    ang_c = col.astype(F32)[:, None] * inv_freq
    cos = jnp.concatenate([jnp.cos(ang_r), jnp.cos(ang_r), jnp.cos(ang_c), jnp.cos(ang_c)], -1)
    sin = jnp.concatenate([-jnp.sin(ang_r), jnp.sin(ang_r), -jnp.sin(ang_c), jnp.sin(ang_c)], -1)
    return cos, sin


def _attention_mixer(h, batch, seq_len, n_pad, cos, sin, mix_norm, w_qkv, w_o, q_norm, k_norm):
    n_heads = w_o.shape[0] // HEAD_DIM
    n_kv = n_heads // GQ
    head_w = jnp.concatenate([jnp.tile(q_norm.astype(F32) * (HEAD_DIM ** -0.5 * LOG2E), n_heads),
                              jnp.tile(k_norm.astype(F32), n_kv),
                              jnp.ones((n_kv * HEAD_DIM,), F32)]).reshape(1, -1)
    qkv = _qkv_proj(h, mix_norm, w_qkv, head_w, cos, sin, seq_len, (n_heads + n_kv) * HEAD_DIM)
    o = _attention(qkv, batch, seq_len, n_heads, n_pad)
    return _matmul_res(o.reshape(batch * seq_len, n_heads * HEAD_DIM), w_o, h, "attn_out")


def kernel(x, meta_tokens, ffn_norm, ffn_w_in, ffn_w_out, mix_norm, ssd_in_proj, ssd_conv_w, ssd_conv_b,
           ssd_dt_bias, ssd_A_log, ssd_D, ssd_norm, ssd_out_proj, attn_w_qkv, attn_q_norm, attn_k_norm,
           attn_w_o):
    batch, n_tok, d = x.shape
    n_meta = meta_tokens.shape[0]
    n_pad = CHUNK - n_meta
    seq_len = n_pad + n_meta + n_tok
    depth = ffn_norm.shape[0]
    cos, sin = _rope_tables(seq_len, n_pad, n_meta)
    h = jnp.concatenate([jnp.zeros((batch, n_pad, d), x.dtype),
                         jnp.broadcast_to(meta_tokens.astype(x.dtype)[None], (batch, n_meta, d)),
                         x], axis=1).reshape(batch * seq_len, d)
    n_dt = 2 * (ssd_out_proj.shape[1] // SSD_HEAD_DIM)
    n_zx = ssd_in_proj.shape[2] - n_dt
    w_in, w_out = ffn_w_in[0, 0].astype(BF16), ffn_w_out[0, 0].astype(BF16)
    for i in range(depth):
        j = i // 2
        if i % 2 == 0:
            mixer_casts = [_Cast(ssd_in_proj, (j,), 0, n_zx), _Cast(ssd_in_proj, (j,), n_zx, n_dt),
                           _Cast(ssd_out_proj, (j,))]
        else:
            mixer_casts = [_Cast(attn_w_qkv, (j,)), _Cast(attn_w_o, (j,))]
        h, cast = _ffn(h, ffn_norm[i, 0], w_in, w_out,
                       mixer_casts + [_Cast(ffn_w_in, (i, 1)), _Cast(ffn_w_out, (i, 1))])
        *mixer_w, w_in, w_out = cast
        if i % 2 == 0:
            h = _ssd_mixer(h, batch, seq_len, n_pad, mix_norm[i], *mixer_w, ssd_conv_w[j], ssd_conv_b[j],
                           ssd_dt_bias[j], ssd_A_log[j], ssd_D[j], ssd_norm[j])
        else:
            h = _attention_mixer(h, batch, seq_len, n_pad, cos, sin, mix_norm[i], *mixer_w,
                                 attn_q_norm[j], attn_k_norm[j])
        last = i + 1 == depth
        h, cast = _ffn(h, ffn_norm[i, 1], w_in, w_out,
                       [] if last else [_Cast(ffn_w_in, (i + 1, 0)), _Cast(ffn_w_out, (i + 1, 0))])
        if not last:
            w_in, w_out = cast
    return h.reshape(batch, seq_len, d)[:, n_pad + n_meta:, :]
```
